```python
import jax, jax.numpy as jnp
from jax import lax

D_MODEL = 2048
BATCH = 2
SEQ = 8192
DEPTH = 1

HEAD_DIM = 64
N_MIX_HEADS = D_MODEL // HEAD_DIM
ATTN_HEADS = 12
RWKV_HEADS = N_MIX_HEADS - ATTN_HEADS
ATTN_DIM = ATTN_HEADS * HEAD_DIM
RWKV_DIM = RWKV_HEADS * HEAD_DIM
DECAY_LORA = 96
ICLR_LORA = 96
GATE_LORA = 256
PROJ_DIM = 3 * RWKV_DIM + DECAY_LORA + ICLR_LORA + GATE_LORA + 3 * ATTN_DIM
DILATION_PAIRS = ((128, 1), (512, 4), (2048, 16))
ATTN_BLOCK = 128
ROPE_THETA = 500000.0
ROT_DIM = HEAD_DIM // 4
D_FF = 5632
ALPHA = (2 * DEPTH) ** 0.25
BETA = (8 * DEPTH) ** -0.25
LN_EPS = 1e-5
GN_EPS = 64e-5

kernel_name = "hymba_rwkv7_dilated_attn_macaron_deepnorm"


def _layer_norm(x, g, b):
    xf = x.astype(jnp.float32)
    mu = jnp.mean(xf, -1, keepdims=True)
    xc = xf - mu
    var = jnp.mean(xc * xc, -1, keepdims=True)
    return (xc * lax.rsqrt(var + LN_EPS) * g + b).astype(x.dtype)


def _swiglu(x, w_gate, w_up, w_down):
    return (jax.nn.silu(x @ w_gate) * (x @ w_up)) @ w_down


def _token_shift(z, mu):
    prev = jnp.pad(z, ((0, 0), (1, 0), (0, 0)))[:, :-1]
    return z + (prev - z) * mu


def _partial_rope(x, positions):
    half = ROT_DIM // 2
    inv_freq = jnp.power(ROPE_THETA, -jnp.arange(half, dtype=jnp.float32) * (2.0 / ROT_DIM))
    ang = positions.astype(jnp.float32)[:, :, None, None] * inv_freq
    cos, sin = jnp.cos(ang), jnp.sin(ang)
    x1, x2, rest = x[..., :half], x[..., half:ROT_DIM], x[..., ROT_DIM:]
    return jnp.concatenate([x1 * cos - x2 * sin, x2 * cos + x1 * sin, rest], -1)


def _wkv7_scan(r, w, k, v, a, b):
    B_, T_, H_, N_ = r.shape

    def step(S, inp):
        r_t, w_t, k_t, v_t, a_t, b_t = inp
        S = (S * w_t[:, :, None, :]
             + jnp.einsum('bhij,bhj->bhi', S, a_t)[..., None] * b_t[:, :, None, :]
             + v_t[..., :, None] * k_t[..., None, :])
        return S, jnp.einsum('bhij,bhj->bhi', S, r_t)

    S0 = jnp.zeros((B_, H_, N_, N_), jnp.float32)
    xs = tuple(jnp.moveaxis(t, 1, 0) for t in (r, w, k, v, a, b))
    _, y = lax.scan(step, S0, xs)
    return jnp.moveaxis(y, 0, 1)


def _rwkv7_mix(r, k, v, w_lo, a_lo, g_lo, mu_r, mu_k, mu_v, mu_w, mu_a, mu_g,
               w0, w2, a0, a2, g2, k_k, k_a, r_k, gn_g, gn_b):
    f32 = jnp.float32
    B_, T_, _ = r.shape
    r = _token_shift(r, mu_r.astype(f32))
    k = _token_shift(k, mu_k.astype(f32))
    v = _token_shift(v, mu_v.astype(f32))
    w_lo = _token_shift(w_lo, mu_w.astype(f32))
    a_lo = _token_shift(a_lo, mu_a.astype(f32))
    g_lo = _token_shift(g_lo, mu_g.astype(f32))
    w = -jax.nn.softplus(-(w0.astype(f32) + jnp.tanh(w_lo) @ w2.astype(f32))) - 0.5
    decay = jnp.exp(-jnp.exp(w))
    a = jax.nn.sigmoid(a0.astype(f32) + a_lo @ a2.astype(f32))
    g = jax.nn.sigmoid(g_lo) @ g2.astype(f32)

    def heads(t):
        return t.reshape(B_, T_, RWKV_HEADS, HEAD_DIM)

    kk = heads(k * k_k.astype(f32))
    kk = kk / jnp.maximum(jnp.sqrt(jnp.sum(kk * kk, -1, keepdims=True)), 1e-12)
    k = k * (1.0 + (a - 1.0) * k_a.astype(f32))
    rh, kh, vh, ah = heads(r), heads(k), heads(v), heads(a)
    y = _wkv7_scan(rh, heads(decay), kh, vh, -kk, kk * ah)
    mu = jnp.mean(y, -1, keepdims=True)
    yc = y - mu
    var = jnp.mean(yc * yc, -1, keepdims=True)
    yn = (yc * lax.rsqrt(var + GN_EPS)).reshape(B_, T_, RWKV_DIM) * gn_g.astype(f32) + gn_b.astype(f32)
    bonus = jnp.sum(rh * kh * r_k.astype(f32), -1, keepdims=True) * vh
    return (yn + bonus.reshape(B_, T_, RWKV_DIM)) * g


def _dilated_branch(q, k, v, window, dilation):
    B_, H_, S_, E_ = q.shape
    span = window // dilation
    unit = dilation * ATTN_BLOCK
    Sp = -(-S_ // unit) * unit
    L = Sp // dilation
    nb = L // ATTN_BLOCK
    pad = ((0, 0), (0, 0), (0, Sp - S_), (0, 0))

    def to_blocks(t):
        t = jnp.pad(t, pad).reshape(B_, H_, L, dilation, E_)
        return jnp.swapaxes(t, 2, 3).reshape(B_, H_, dilation, nb, ATTN_BLOCK, E_)

    def with_prev(t):
        prev = jnp.concatenate([jnp.zeros_like(t[:, :, :, :1]), t[:, :, :, :-1]], axis=3)
        return jnp.concatenate([prev, t], axis=4)

    qb = to_blocks(q)
    kw = with_prev(to_blocks(k))
    vw = with_prev(to_blocks(v))
    s = jnp.einsum('bhrnqe,bhrnke->bhrnqk', qb, kw) * (E_ ** -0.5)
    qi = jnp.arange(ATTN_BLOCK)[:, None] + ATTN_BLOCK
    kj = jnp.arange(2 * ATTN_BLOCK)[None, :]
    rel = qi - kj
    band = (rel >= 0) & (rel <= span)
    valid = band[None] & ((jnp.arange(nb)[:, None, None] > 0) | (kj[None] >= ATTN_BLOCK))
    s = jnp.where(valid, s, -jnp.inf)
    m = jnp.max(s, -1, keepdims=True)
    p = jnp.exp(s - m)
    den = jnp.sum(p, -1, keepdims=True)
    o = jnp.einsum('bhrnqk,bhrnke->bhrnqe', p, vw) / den
    lse = (m + jnp.log(den))[..., 0]
    o = jnp.swapaxes(o.reshape(B_, H_, dilation, L, E_), 2, 3).reshape(B_, H_, Sp, E_)[:, :, :S_]
    lse = jnp.swapaxes(lse.reshape(B_, H_, dilation, L), 2, 3).reshape(B_, H_, Sp)[:, :, :S_]
    return o, lse


def _dilated_attention(q, k, v, positions):
    B_, T_, _ = q.shape

    def heads(t):
        return t.reshape(B_, T_, ATTN_HEADS, HEAD_DIM)

    q = jnp.swapaxes(_partial_rope(heads(q), positions), 1, 2)
    k = jnp.swapaxes(_partial_rope(heads(k), positions), 1, 2)
    v = jnp.swapaxes(heads(v), 1, 2)
    outs, lses = zip(*[_dilated_branch(q, k, v, w, d) for (w, d) in DILATION_PAIRS])
    wts = jax.nn.softmax(jnp.stack(lses), axis=0)
    o = jnp.einsum('gbht,gbhte->bthe', wts, jnp.stack(outs))
    return o.reshape(B_, T_, ATTN_DIM)


def _hybrid_mixer(h, positions, w_in, mu_r, mu_k, mu_v, mu_w, mu_a, mu_g, w0, w2, a0, a2, g2,
                  k_k, k_a, r_k, gn_g, gn_b, w_out):
    proj = (h @ w_in).astype(jnp.float32)
    widths = (RWKV_DIM, RWKV_DIM, RWKV_DIM, DECAY_LORA, ICLR_LORA, GATE_LORA, ATTN_DIM, ATTN_DIM, ATTN_DIM)
    cuts = [sum(widths[:i + 1]) for i in range(len(widths) - 1)]
    r, k, v, w_lo, a_lo, g_lo, qa, ka, va = jnp.split(proj, cuts, axis=-1)
    y_rwkv = _rwkv7_mix(r, k, v, w_lo, a_lo, g_lo, mu_r, mu_k, mu_v, mu_w, mu_a, mu_g,
                        w0, w2, a0, a2, g2, k_k, k_a, r_k, gn_g, gn_b)
    y_attn = _dilated_attention(qa, ka, va, positions)
    y = jnp.concatenate([y_rwkv, y_attn], -1).astype(h.dtype)
    return y @ w_out


def setup_inputs(seed: int = 0) -> dict:
    key = jax.random.key(seed)
    keys = jax.random.split(key, 32)
    counter = iter(range(32))
    f32 = jnp.float32

    def nk():
        return keys[next(counter)]

    def normal(shape, scale):
        return jax.random.normal(nk(), shape, f32) * scale

    def unif(shape, lo, hi):
        return jax.random.uniform(nk(), shape, f32, lo, hi)

    Ld = DEPTH
    x = normal((BATCH, SEQ, D_MODEL), 1.0)
    offsets = jax.random.randint(nk(), (BATCH, 1), 0, 1024)
    positions = (offsets + jnp.arange(SEQ)[None, :]).astype(jnp.int32)
    return {
        "x": x,
        "positions": positions,
        "ffn1_w_gate": normal((Ld, D_MODEL, D_FF), D_MODEL ** -0.5),
        "ffn1_w_up": normal((Ld, D_MODEL, D_FF), D_MODEL ** -0.5),
        "ffn1_w_down": normal((Ld, D_FF, D_MODEL), BETA * D_FF ** -0.5),
        "ln1_g": 1.0 + normal((Ld, D_MODEL), 0.02),
        "ln1_b": normal((Ld, D_MODEL), 0.02),
        "w_in": normal((Ld, D_MODEL, PROJ_DIM), D_MODEL ** -0.5),
        "mu_r": unif((Ld, RWKV_DIM), 0.0, 1.0),
        "mu_k": unif((Ld, RWKV_DIM), 0.0, 1.0),
        "mu_v": unif((Ld, RWKV_DIM), 0.0, 1.0),
        "mu_w": unif((Ld, DECAY_LORA), 0.0, 1.0),
        "mu_a": unif((Ld, ICLR_LORA), 0.0, 1.0),
        "mu_g": unif((Ld, GATE_LORA), 0.0, 1.0),
        "w0": unif((Ld, RWKV_DIM), -6.5, -1.5),
        "w2": normal((Ld, DECAY_LORA, RWKV_DIM), 0.5 * DECAY_LORA ** -0.5),
        "a0": normal((Ld, RWKV_DIM), 0.1),
        "a2": normal((Ld, ICLR_LORA, RWKV_DIM), ICLR_LORA ** -0.5),
        "g2": normal((Ld, GATE_LORA, RWKV_DIM), GATE_LORA ** -0.5),
        "k_k": 0.85 + normal((Ld, RWKV_DIM), 0.05),
        "k_a": 1.0 + normal((Ld, RWKV_DIM), 0.05),
        "r_k": normal((Ld, RWKV_HEADS, HEAD_DIM), 0.1),
        "gn_g": 1.0 + normal((Ld, RWKV_DIM), 0.02),
        "gn_b": normal((Ld, RWKV_DIM), 0.02),
        "w_out": normal((Ld, D_MODEL, D_MODEL), BETA * D_MODEL ** -0.5),
        "ln2_g": 1.0 + normal((Ld, D_MODEL), 0.02),
        "ln2_b": normal((Ld, D_MODEL), 0.02),
        "ffn2_w_gate": normal((Ld, D_MODEL, D_FF), D_MODEL ** -0.5),
        "ffn2_w_up": normal((Ld, D_MODEL, D_FF), D_MODEL ** -0.5),
        "ffn2_w_down": normal((Ld, D_FF, D_MODEL), BETA * D_FF ** -0.5),
        "ln3_g": 1.0 + normal((Ld, D_MODEL), 0.02),
        "ln3_b": normal((Ld, D_MODEL), 0.02),
    }


def reference(x, positions, ffn1_w_gate, ffn1_w_up, ffn1_w_down, ln1_g, ln1_b, w_in,
              mu_r, mu_k, mu_v, mu_w, mu_a, mu_g, w0, w2, a0, a2, g2, k_k, k_a, r_k,
              gn_g, gn_b, w_out, ln2_g, ln2_b, ffn2_w_gate, ffn2_w_up, ffn2_w_down, ln3_g, ln3_b):
    h = x
    for l in range(DEPTH):
        h = _layer_norm(ALPHA * h + 0.5 * _swiglu(h, ffn1_w_gate[l], ffn1_w_up[l], ffn1_w_down[l]),
                        ln1_g[l], ln1_b[l])
        mix = _hybrid_mixer(h, positions, w_in[l], mu_r[l], mu_k[l], mu_v[l], mu_w[l], mu_a[l], mu_g[l],
                            w0[l], w2[l], a0[l], a2[l], g2[l], k_k[l], k_a[l], r_k[l],
                            gn_g[l], gn_b[l], w_out[l])
        h = _layer_norm(ALPHA * h + mix, ln2_g[l], ln2_b[l])
        h = _layer_norm(ALPHA * h + 0.5 * _swiglu(h, ffn2_w_gate[l], ffn2_w_up[l], ffn2_w_down[l]),
                        ln3_g[l], ln3_b[l])
    return h
```

```python
import functools

import jax
import jax.numpy as jnp
from jax import lax
from jax.experimental import pallas as pl
from jax.experimental.pallas import tpu as pltpu

F32 = jnp.float32
BF16 = jnp.bfloat16

D_MODEL = 2048
HEAD_DIM = 64
ATTN_HEADS = 12
RWKV_HEADS = 20
ATTN_DIM = ATTN_HEADS * HEAD_DIM
RWKV_DIM = RWKV_HEADS * HEAD_DIM
DECAY_LORA = 96
ICLR_LORA = 96
GATE_LORA = 256
DILATIONS = (1, 4, 16)
ATTN_BLOCK = 128
ROPE_THETA = 500000.0
ROT_DIM = HEAD_DIM // 4
D_FF = 5632
DEPTH = 1
ALPHA = (2 * DEPTH) ** 0.25
LN_EPS = 1e-5
GN_EPS = 64e-5

LANES = 128
LORA_PAD = 512
CHUNK = 64
NEG_BIG = -1e30
HI = lax.Precision.HIGHEST


def _cparams(sem, vmem_mb):
    return pltpu.CompilerParams(dimension_semantics=sem, vmem_limit_bytes=vmem_mb * 1024 * 1024)


def _layer_norm(z, g, b):
    mu = jnp.mean(z, axis=-1, keepdims=True)
    zc = z - mu
    var = jnp.mean(zc * zc, axis=-1, keepdims=True)
    return zc * lax.rsqrt(var + LN_EPS) * g + b


def _ffn_ln_body(x_ref, wg_ref, wu_ref, wd_ref, g_ref, b_ref, o_ref, xb_ref, acc_ref, *, nf):
    j = pl.program_id(1)

    @pl.when(j == 0)
    def _():
        xb_ref[...] = x_ref[...].astype(BF16)
        acc_ref[...] = jnp.zeros_like(acc_ref)

    xb = xb_ref[...]
    gate = jnp.dot(xb, wg_ref[...], preferred_element_type=F32)
    up = jnp.dot(xb, wu_ref[...], preferred_element_type=F32)
    hmid = (gate * jax.nn.sigmoid(gate)) * up
    acc_ref[...] += jnp.dot(hmid.astype(BF16), wd_ref[...], preferred_element_type=F32)

    @pl.when(j == nf - 1)
    def _():
        z = ALPHA * x_ref[...] + 0.5 * acc_ref[...]
        o_ref[...] = _layer_norm(z, g_ref[...], b_ref[...])


def _ffn_ln(x, wg, wu, wd, g, b, *, tm=512, tf=512):
    m, d = x.shape
    dff = wg.shape[1]
    nf = dff // tf
    return pl.pallas_call(
        functools.partial(_ffn_ln_body, nf=nf),
        grid=(m // tm, nf),
        in_specs=[
            pl.BlockSpec((tm, d), lambda i, j: (i, 0)),
            pl.BlockSpec((d, tf), lambda i, j: (0, j)),
            pl.BlockSpec((d, tf), lambda i, j: (0, j)),
            pl.BlockSpec((tf, d), lambda i, j: (j, 0)),
            pl.BlockSpec((1, d), lambda i, j: (0, 0)),
            pl.BlockSpec((1, d), lambda i, j: (0, 0)),
        ],
        out_specs=pl.BlockSpec((tm, d), lambda i, j: (i, 0)),
        out_shape=jax.ShapeDtypeStruct((m, d), F32),
        scratch_shapes=[pltpu.VMEM((tm, d), BF16), pltpu.VMEM((tm, d), F32)],
        compiler_params=_cparams(("parallel", "arbitrary"), 48),
        name="ffn_ln",
    )(x, wg, wu, wd, g, b)


def _proj_body(x_ref, w_ref, o_ref, xb_ref):
    @pl.when(pl.program_id(1) == 0)
    def _():
        xb_ref[...] = x_ref[...].astype(BF16)

    o_ref[...] = jnp.dot(xb_ref[...], w_ref[...], preferred_element_type=F32)


def _proj(x, w, *, tm=512, tn=None, name="proj"):
    m, d = x.shape
    n = w.shape[1]
    tn = tn or n
    return pl.pallas_call(
        _proj_body,
        grid=(m // tm, n // tn),
        in_specs=[
            pl.BlockSpec((tm, d), lambda i, j: (i, 0)),
            pl.BlockSpec((d, tn), lambda i, j: (0, j)),
        ],
        out_specs=pl.BlockSpec((tm, tn), lambda i, j: (i, j)),
        out_shape=jax.ShapeDtypeStruct((m, n), F32),
        scratch_shapes=[pltpu.VMEM((tm, d), BF16)],
        compiler_params=_cparams(("parallel", "arbitrary"), 40),
        name=name,
    )(x, w)


def _rope_tables_body(pos_ref, invf_ref, rotm_ref, cos_ref, sin_ref):
    ang = pos_ref[...] * invf_ref[...]
    rot = rotm_ref[...] > 0.0
    cos_ref[...] = jnp.where(rot, jnp.cos(ang), 1.0)
    sin_ref[...] = jnp.where(rot, jnp.sin(ang), 0.0)


def _rope_tables(pos_f32, *, tm=1024):
    m = pos_f32.shape[0]
    half = ROT_DIM // 2
    inv_freq = jnp.power(ROPE_THETA, -jnp.arange(half, dtype=F32) * (2.0 / ROT_DIM))
    l64 = jnp.arange(LANES) % HEAD_DIM
    invf = jnp.where(l64 < ROT_DIM, inv_freq[l64 % half], 0.0).astype(F32)[None, :]
    rotm = (l64 < ROT_DIM).astype(F32)[None, :]
    return pl.pallas_call(
        _rope_tables_body,
        grid=(m // tm,),
        in_specs=[
            pl.BlockSpec((tm, 1), lambda i: (i, 0)),
            pl.BlockSpec((1, LANES), lambda i: (0, 0)),
            pl.BlockSpec((1, LANES), lambda i: (0, 0)),
        ],
        out_specs=[pl.BlockSpec((tm, LANES), lambda i: (i, 0))] * 2,
        out_shape=[jax.ShapeDtypeStruct((m, LANES), F32)] * 2,
        compiler_params=_cparams(("parallel",), 16),
        name="rope_tables",
    )(pos_f32, invf, rotm)


def _qkv_rope_body(x_ref, w_ref, cos_ref, sin_ref, o_ref, xb_ref):
    j = pl.program_id(1)

    @pl.when(j == 0)
    def _():
        xb_ref[...] = x_ref[...].astype(BF16)

    y = jnp.dot(xb_ref[...], w_ref[...], preferred_element_type=F32)

    @pl.when(j == 2)
    def _():
        o_ref[...] = y.astype(BF16)

    @pl.when(j < 2)
    def _():
        c = cos_ref[...]
        s = sin_ref[...]
        l64 = lax.broadcasted_iota(jnp.int32, (1, LANES), 1) & (HEAD_DIM - 1)
        lo = l64 < ROT_DIM // 2
        hi = l64 < ROT_DIM
        scale = jnp.where(j == 0, HEAD_DIM ** -0.5, 1.0).astype(F32)
        for t in range(ATTN_DIM // LANES):
            yt = y[:, t * LANES:(t + 1) * LANES]
            up = pltpu.roll(yt, LANES - ROT_DIM // 2, axis=1)
            dn = pltpu.roll(yt, ROT_DIM // 2, axis=1)
            rot = jnp.where(lo, -up, jnp.where(hi, dn, 0.0))
            o_ref[:, t * LANES:(t + 1) * LANES] = ((yt * c + rot * s) * scale).astype(BF16)


def _qkv_rope(x, w3, cos_t, sin_t, *, tm=512):
    m, d = x.shape
    n = w3.shape[2]
    return pl.pallas_call(
        _qkv_rope_body,
        grid=(m // tm, 3),
        in_specs=[
            pl.BlockSpec((tm, d), lambda i, j: (i, 0)),
            pl.BlockSpec((None, d, n), lambda i, j: (j, 0, 0)),
            pl.BlockSpec((tm, LANES), lambda i, j: (i, 0)),
            pl.BlockSpec((tm, LANES), lambda i, j: (i, 0)),
        ],
        out_specs=pl.BlockSpec((None, tm, n), lambda i, j: (j, i, 0)),
        out_shape=jax.ShapeDtypeStruct((3, m, n), BF16),
        scratch_shapes=[pltpu.VMEM((tm, d), BF16)],
        compiler_params=_cparams(("parallel", "arbitrary"), 32),
        name="qkv_rope",
    )(x, w3, cos_t, sin_t)


def _rwkv_prep_body(r_ref, k_ref, v_ref, l_ref, mur_ref, muk_ref, muv_ref, mul_ref,
                    w0_ref, w2_ref, a0_ref, a2_ref,
                    ro_ref, ko_ref, vo_ref, lw_ref, ao_ref, go_ref,
                    cr_ref, ck_ref, cv_ref, cl_ref, *, tiles_per_seq):
    i = pl.program_id(0)
    tm = r_ref.shape[0]
    row0 = lax.broadcasted_iota(jnp.int32, (tm, 1), 0) == 0

    @pl.when((i % tiles_per_seq) == 0)
    def _():
        for c_ref in (cr_ref, ck_ref, cv_ref, cl_ref):
            c_ref[...] = jnp.zeros_like(c_ref)

    def shift(z_ref, carry_ref, mu_ref):
        z = z_ref[...]
        prev = jnp.where(row0, carry_ref[0:1, :], pltpu.roll(z, 1, axis=0))
        carry_ref[0:1, :] = z[tm - 1:tm, :]
        return z + (prev - z) * mu_ref[...]

    ro_ref[...] = shift(r_ref, cr_ref, mur_ref)
    ko_ref[...] = shift(k_ref, ck_ref, muk_ref)
    vo_ref[...] = shift(v_ref, cv_ref, muv_ref)
    ls = shift(l_ref, cl_ref, mul_ref)
    w_lo = ls[:, 0:LANES]
    a_lo = ls[:, LANES:2 * LANES]
    go_ref[...] = ls[:, 2 * LANES:]
    w = w0_ref[...] + jnp.dot(jnp.tanh(w_lo), w2_ref[...], precision=HI, preferred_element_type=F32)
    x = -w
    softplus = jnp.maximum(x, 0.0) + jnp.log(1.0 + jnp.exp(-jnp.abs(x)))
    lw_ref[...] = -jnp.exp(-softplus - 0.5)
    a = a0_ref[...] + jnp.dot(a_lo, a2_ref[...], precision=HI, preferred_element_type=F32)
    ao_ref[...] = jax.nn.sigmoid(a)


def _rwkv_prep(rkv, lora, mu_r, mu_k, mu_v, mu_l, w0, w2p, a0, a2p, *, seq, tm=256):
    m = rkv.shape[0]
    n = RWKV_DIM
    row = lambda w: pl.BlockSpec((1, w), lambda i: (0, 0))
    return pl.pallas_call(
        functools.partial(_rwkv_prep_body, tiles_per_seq=seq // tm),
        grid=(m // tm,),
        in_specs=[
            pl.BlockSpec((tm, n), lambda i: (i, 0)),
            pl.BlockSpec((tm, n), lambda i: (i, 1)),
            pl.BlockSpec((tm, n), lambda i: (i, 2)),
            pl.BlockSpec((tm, LORA_PAD), lambda i: (i, 0)),
            row(n), row(n), row(n), row(LORA_PAD),
            row(n), pl.BlockSpec((LANES, n), lambda i: (0, 0)),
            row(n), pl.BlockSpec((LANES, n), lambda i: (0, 0)),
        ],
        out_specs=[pl.BlockSpec((tm, n), lambda i: (i, 0))] * 5
        + [pl.BlockSpec((tm, GATE_LORA), lambda i: (i, 0))],
        out_shape=[jax.ShapeDtypeStruct((m, n), F32)] * 5
        + [jax.ShapeDtypeStruct((m, GATE_LORA), F32)],
        scratch_shapes=[pltpu.VMEM((8, n), F32)] * 3 + [pltpu.VMEM((8, LORA_PAD), F32)],
        compiler_params=_cparams(("arbitrary",), 40),
        name="rwkv_prep",
    )(rkv, rkv, rkv, lora, mu_r, mu_k, mu_v, mu_l, w0, w2p, a0, a2p)


_NN = (((1,), (0,)), ((), ()))
_NT = (((1,), (1,)), ((), ()))
_TN = (((0,), (0,)), ((), ()))


def _split(a):
    hi = a.astype(BF16)
    return hi, (a - hi.astype(F32)).astype(BF16)


def _mm(a, b, dims=_NN):
    ah, al = _split(a)
    bh, bl = _split(b)
    d = lambda x, y: lax.dot_general(x, y, dims, preferred_element_type=F32)
    return d(ah, bh) + (d(ah, bl) + d(al, bh))


def _cumsum_rows(tri, x):
    hi, lo = _split(x)
    lo2 = (x - hi.astype(F32) - lo.astype(F32)).astype(BF16)
    d = lambda y: jnp.dot(tri, y, preferred_element_type=F32)
    return d(hi) + (d(lo) + d(lo2))


def _wkv_body(r_ref, lw_ref, k_ref, v_ref, a_ref, kk_ref, ka_ref, rk_ref, gg_ref, gb_ref,
              y_ref, g_ref):
    C = CHUNK
    tc = r_ref.shape[0]

    @pl.when(pl.program_id(2) == 0)
    def _():
        g_ref[...] = jnp.zeros_like(g_ref)

    lane = lax.broadcasted_iota(jnp.int32, (1, LANES), 1)
    m0 = (lane < HEAD_DIM).astype(F32)
    m1 = 1.0 - m0
    h0 = lane < HEAD_DIM

    ri = lax.broadcasted_iota(jnp.int32, (2 * C, 2 * C), 0)
    ci = lax.broadcasted_iota(jnp.int32, (2 * C, 2 * C), 1)
    ti, sj = ri & (C - 1), ci & (C - 1)
    top, left = ri < C, ci < C
    mask0 = (sj < ti) | ((~top) & (sj == ti))
    mask1 = (sj < ti) | (top & (sj == ti))
    quad_tl = top & left
    quad_br = (~top) & (~left)
    eye = ri == ci
    same_head = (ri < HEAD_DIM) == (ci < HEAD_DIM)
    level_masks = []
    for lg in range(C.bit_length() - 1):
        level_masks.append(((ri >> (lg + 1)) == (ci >> (lg + 1)))
                           & (((ri >> lg) & 1) == 1) & (((ci >> lg) & 1) == 0))
    tri = jnp.where(lax.broadcasted_iota(jnp.int32, (C, C), 1) <= lax.broadcasted_iota(jnp.int32, (C, C), 0),
                    1.0, 0.0).astype(BF16)
    zeros_c = jnp.zeros((C, LANES), F32)

    def segsum(x):
        s0 = jnp.sum(x * m0, axis=-1, keepdims=True)
        s1 = jnp.sum(x * m1, axis=-1, keepdims=True)
        return jnp.where(h0, s0, s1)

    kk_p, ka_p, rk_p = kk_ref[...], ka_ref[...], rk_ref[...]
    gn_g, gn_b = gg_ref[...], gb_ref[...]

    def chunk(ic, G):
        rows = pl.ds(pl.multiple_of(ic * C, C), C)
        r, lw, ks, v, a = r_ref[rows, :], lw_ref[rows, :], k_ref[rows, :], v_ref[rows, :], a_ref[rows, :]

        kk = ks * kk_p
        kk = kk / jnp.maximum(jnp.sqrt(segsum(kk * kk)), 1e-12)
        kmod = ks * (1.0 + (a - 1.0) * ka_p)
        av = -kk
        bv = kk * a

        cs = _cumsum_rows(tri, lw)
        cl = cs[C - 1:C, :]
        At = av * jnp.exp(cs - lw)
        Rt = r * jnp.exp(cs)
        pinv = jnp.exp(-cs)
        Bt = bv * pinv
        Kt = kmod * pinv
        pend = jnp.exp(cl - cs)
        Bh = bv * pend
        Kh = kmod * pend

        M0 = _mm(jnp.concatenate([At * m0, Rt * m0], 0), jnp.concatenate([Bt, Kt], 0), _NT)
        M1 = _mm(jnp.concatenate([Rt * m1, At * m1], 0), jnp.concatenate([Kt, Bt], 0), _NT)
        M0 = jnp.where(mask0, M0, 0.0)
        M1 = jnp.where(mask1, M1, 0.0)

        Lbd = jnp.where(quad_tl, M0, 0.0) + jnp.where(quad_br, M1, 0.0)
        X = jnp.where(eye, 1.0, 0.0) + jnp.where(level_masks[0], Lbd, 0.0)
        for lm in level_masks[1:]:
            X = X + _mm(_mm(X, jnp.where(lm, Lbd, 0.0)), X)

        v0, v1 = v * m0, v * m1
        W = _mm(jnp.concatenate([M0[:C], M1[C:]], 1), jnp.concatenate([zeros_c, v0, v1, zeros_c], 0))
        rhs = jnp.concatenate([
            jnp.concatenate([At * m0, W * m0], 1),
            jnp.concatenate([At * m1, W * m1], 1)], 0)
        res = _mm(X, rhs)
        Ap = res[:C, :LANES] + res[C:, :LANES]
        XW = res[:C, LANES:] + res[C:, LANES:]

        rhs_y = jnp.concatenate([
            jnp.concatenate([Ap * m0, XW * m0], 1),
            jnp.concatenate([zeros_c, v0], 1),
            jnp.concatenate([zeros_c, v1], 1),
            jnp.concatenate([Ap * m1, XW * m1], 1)], 0)
        res_y = _mm(jnp.concatenate([M0[C:], M1[:C]], 1), rhs_y)
        Rp = Rt + res_y[:, :LANES]
        Y0 = res_y[:, LANES:]

        lhs_t = jnp.concatenate([
            jnp.concatenate([Ap, XW], 1),
            jnp.concatenate([zeros_c, v], 1)], 0)
        res_t = _mm(lhs_t, jnp.concatenate([Bh, Kh], 0), _TN)
        Tm = jnp.where(same_head, res_t[:LANES] + jnp.where(eye, jnp.exp(cl), 0.0), 0.0)
        G0 = jnp.where(same_head, res_t[LANES:], 0.0)

        y = _mm(Rp, G, _NT) + Y0
        G_new = _mm(G, Tm) + G0

        mu = segsum(y) * (1.0 / HEAD_DIM)
        yc = y - mu
        var = segsum(yc * yc) * (1.0 / HEAD_DIM)
        yn = yc * lax.rsqrt(var + GN_EPS) * gn_g + gn_b
        bonus = segsum(r * kmod * rk_p) * v
        y_ref[rows, :] = yn + bonus
        return G_new

    g_ref[...] = lax.fori_loop(0, tc // C, chunk, g_ref[...])


def _wkv(r, lw, k, v, a, k_k, k_a, r_k, gn_g, gn_b, *, batch, seq, tc=512):
    n = RWKV_DIM
    shp = (batch, seq, n)
    seq_spec = pl.BlockSpec((None, tc, LANES), lambda b, p, i: (b, i, p))
    par_spec = pl.BlockSpec((1, LANES), lambda b, p, i: (0, p))
    out = pl.pallas_call(
        _wkv_body,
        grid=(batch, n // LANES, seq // tc),
        in_specs=[seq_spec] * 5 + [par_spec] * 5,
        out_specs=seq_spec,
        out_shape=jax.ShapeDtypeStruct(shp, F32),
        scratch_shapes=[pltpu.VMEM((LANES, LANES), F32)],
        compiler_params=_cparams(("parallel", "parallel", "arbitrary"), 32),
        name="wkv",
    )(r.reshape(shp), lw.reshape(shp), k.reshape(shp), v.reshape(shp), a.reshape(shp),
      k_k, k_a, r_k, gn_g, gn_b)
    return out.reshape(batch * seq, n)


def _dil_attn_body(q_ref, kc_ref, vc_ref, kp_ref, vp_ref, o_ref, lse_ref):
    blk = ATTN_BLOCK
    tt = q_ref.shape[0]
    first_lo = jnp.where(pl.program_id(3) == 0, blk, 0)

    lane = lax.broadcasted_iota(jnp.int32, (1, LANES), 1)
    h0 = lane < HEAD_DIM
    qi = lax.broadcasted_iota(jnp.int32, (blk, 2 * blk), 0)
    kj = lax.broadcasted_iota(jnp.int32, (blk, 2 * blk), 1)
    band = (kj >= qi) & (kj <= qi + blk)

    for j in range(tt // blk):
        q = q_ref[j * blk:(j + 1) * blk, :]
        if j == 0:
            keys = jnp.concatenate([kp_ref[...], kc_ref[0:blk, :]], 0)
            vals = jnp.concatenate([vp_ref[...], vc_ref[0:blk, :]], 0)
            valid = band & (kj >= first_lo)
        else:
            keys = kc_ref[(j - 1) * blk:(j + 1) * blk, :]
            vals = vc_ref[(j - 1) * blk:(j + 1) * blk, :]
            valid = band
        zero = jnp.zeros_like(q)
        ps, dens, ms = [], [], []
        for hm in (h0, jnp.logical_not(h0)):
            s = lax.dot_general(jnp.where(hm, q, zero), keys, _NT, preferred_element_type=F32)
            s = jnp.where(valid, s, NEG_BIG)
            m = jnp.max(s, axis=-1, keepdims=True)
            p = jnp.exp(s - m)
            dens.append(jnp.sum(p, axis=-1, keepdims=True))
            ms.append(m)
            ps.append(p.astype(BF16))
        zv = jnp.zeros_like(vals)
        vbd = jnp.concatenate([jnp.where(h0, vals, zv), jnp.where(h0, zv, vals)], 0)
        o = jnp.dot(jnp.concatenate(ps, 1), vbd, preferred_element_type=F32)
        den = jnp.where(h0, dens[0], dens[1])
        o_ref[j * blk:(j + 1) * blk, :] = o / den
        lse_ref[j * blk:(j + 1) * blk, :] = jnp.where(h0, ms[0], ms[1]) + jnp.log(den)


def _dil_attn(qkv, *, batch, seq, dil, tt=512):
    ld = seq // dil
    tt = min(tt, ld)
    npair = ATTN_DIM // LANES
    x = qkv.reshape(3, batch, ld, dil * ATTN_DIM)
    nb = tt // ATTN_BLOCK

    def cur(which):
        return pl.BlockSpec((None, None, tt, LANES), lambda b, r, p, i: (which, b, i, r * npair + p))

    def prev(which):
        return pl.BlockSpec((None, None, ATTN_BLOCK, LANES),
                            lambda b, r, p, i: (which, b, jnp.maximum(i * nb - 1, 0), r * npair + p))

    out_spec = pl.BlockSpec((None, tt, LANES), lambda b, r, p, i: (b, i, r * npair + p))
    o, lse = pl.pallas_call(
        _dil_attn_body,
        grid=(batch, dil, npair, ld // tt),
        in_specs=[cur(0), cur(1), cur(2), prev(1), prev(2)],
        out_specs=[out_spec, out_spec],
        out_shape=[jax.ShapeDtypeStruct((batch, ld, dil * ATTN_DIM), F32)] * 2,
        compiler_params=_cparams(("parallel", "parallel", "parallel", "arbitrary"), 32),
        name=f"dil_attn_d{dil}",
    )(x, x, x, x, x)
    return o.reshape(batch * seq, ATTN_DIM), lse.reshape(batch * seq, ATTN_DIM)


def _out_ln_body(yw_ref, gl_ref, o1_ref, o2_ref, o3_ref, l1_ref, l2_ref, l3_ref, h_ref,
                 g2_ref, wr_ref, wa_ref, g_ref, b_ref, out_ref):
    gate = jnp.dot(jax.nn.sigmoid(gl_ref[...]).astype(BF16), g2_ref[...], preferred_element_type=F32)
    yr = yw_ref[...] * gate
    l1, l2, l3 = l1_ref[...], l2_ref[...], l3_ref[...]
    mx = jnp.maximum(jnp.maximum(l1, l2), l3)
    e1, e2, e3 = jnp.exp(l1 - mx), jnp.exp(l2 - mx), jnp.exp(l3 - mx)
    den = e1 + e2 + e3
    ya = (e1 / den) * o1_ref[...] + (e2 / den) * o2_ref[...] + (e3 / den) * o3_ref[...]
    mix = (jnp.dot(yr.astype(BF16), wr_ref[...], preferred_element_type=F32)
           + jnp.dot(ya.astype(BF16), wa_ref[...], preferred_element_type=F32))
    z = ALPHA * h_ref[...] + mix
    out_ref[...] = _layer_norm(z, g_ref[...], b_ref[...])


def _out_ln(yw, glo, os_, lses, h, g2, wo_r, wo_a, g, b, *, tm=256):
    m, d = h.shape
    rows = lambda w: pl.BlockSpec((tm, w), lambda i: (i, 0))
    full = lambda a: pl.BlockSpec(a.shape, lambda i: (0, 0))
    return pl.pallas_call(
        _out_ln_body,
        grid=(m // tm,),
        in_specs=[rows(RWKV_DIM), rows(GATE_LORA)] + [rows(ATTN_DIM)] * 6 + [rows(d),
                  full(g2), full(wo_r), full(wo_a), full(g), full(b)],
        out_specs=rows(d),
        out_shape=jax.ShapeDtypeStruct((m, d), F32),
        compiler_params=_cparams(("parallel",), 48),
        name="out_ln",
    )(yw, glo, *os_, *lses, h, g2, wo_r, wo_a, g, b)


def _pad_rows(w, rows):
    return jnp.pad(w, ((0, rows - w.shape[0]), (0, 0)))


def _pad_cols(w, cols):
    return jnp.pad(w, ((0, 0), (0, cols - w.shape[1])))


def kernel(x, positions, ffn1_w_gate, ffn1_w_up, ffn1_w_down, ln1_g, ln1_b, w_in, mu_r, mu_k, mu_v, mu_w, mu_a, mu_g, w0, w2, a0, a2, g2, k_k, k_a, r_k, gn_g, gn_b, w_out, ln2_g, ln2_b, ffn2_w_gate, ffn2_w_up, ffn2_w_down, ln3_g, ln3_b):
    batch, seq, d = x.shape
    m = batch * seq
    h = x.reshape(m, d)
    cos_t, sin_t = _rope_tables(positions.astype(F32).reshape(m, 1))
    c = RWKV_DIM
    for l in range(DEPTH):
        row = lambda p: p[l].reshape(1, -1)
        h = _ffn_ln(h, ffn1_w_gate[l].astype(BF16), ffn1_w_up[l].astype(BF16),
                    ffn1_w_down[l].astype(BF16), row(ln1_g), row(ln1_b))

        wi = w_in[l]
        o_lo = 3 * c
        o_q = o_lo + DECAY_LORA + ICLR_LORA + GATE_LORA
        w_rkv = wi[:, :o_lo].astype(BF16)
        w_lora = jnp.concatenate([
            _pad_cols(wi[:, o_lo:o_lo + DECAY_LORA], LANES),
            _pad_cols(wi[:, o_lo + DECAY_LORA:o_lo + DECAY_LORA + ICLR_LORA], LANES),
            wi[:, o_lo + DECAY_LORA + ICLR_LORA:o_q]], axis=1).astype(BF16)
        w_qkv = wi[:, o_q:].reshape(d, 3, ATTN_DIM).transpose(1, 0, 2).astype(BF16)
        mu_l = jnp.concatenate([_pad_cols(row(mu_w), LANES), _pad_cols(row(mu_a), LANES), row(mu_g)], axis=1)

        rkv = _proj(h, w_rkv, tn=c, name="proj_rkv")
        lora = _proj(h, w_lora, name="proj_lora")
        qkv = _qkv_rope(h, w_qkv, cos_t, sin_t)

        r_s, k_s, v_s, lw, a_s, glo = _rwkv_prep(
            rkv, lora, row(mu_r), row(mu_k), row(mu_v), mu_l,
            row(w0), _pad_rows(w2[l], LANES), row(a0), _pad_rows(a2[l], LANES), seq=seq)
        yw = _wkv(r_s, lw, k_s, v_s, a_s, row(k_k), row(k_a), row(r_k), row(gn_g), row(gn_b),
                  batch=batch, seq=seq)

        branches = [_dil_attn(qkv, batch=batch, seq=seq, dil=dil) for dil in DILATIONS]
        os_ = [o for o, _ in branches]
        lses = [s for _, s in branches]

        wo = w_out[l].astype(BF16)
        h = _out_ln(yw, glo, os_, lses, h, g2[l].astype(BF16), wo[:c], wo[c:], row(ln2_g), row(ln2_b))

        h = _ffn_ln(h, ffn2_w_gate[l].astype(BF16), ffn2_w_up[l].astype(BF16),
                    ffn2_w_down[l].astype(BF16), row(ln3_g), row(ln3_b))
    return h.reshape(batch, seq, d)
```

```python
import functools

import numpy as np

import jax
import jax.numpy as jnp
from jax import lax
from jax.experimental import pallas as pl
from jax.experimental.pallas import tpu as pltpu

F32 = jnp.float32
BF16 = jnp.bfloat16

D_MODEL = 2048
HEAD_DIM = 64
ATTN_HEADS = 12
RWKV_HEADS = 20
ATTN_DIM = ATTN_HEADS * HEAD_DIM
RWKV_DIM = RWKV_HEADS * HEAD_DIM
DECAY_LORA = 96
ICLR_LORA = 96
GATE_LORA = 256
DILATIONS = (1, 4, 16)
ATTN_BLOCK = 128
ROPE_THETA = 500000.0
ROT_DIM = HEAD_DIM // 4
D_FF = 5632
DEPTH = 1
ALPHA = (2 * DEPTH) ** 0.25
LN_EPS = 1e-5
GN_EPS = 64e-5

LANES = 128
LORA_PAD = 512
CHUNK = 64
NEG_BIG = -1e30
HI = lax.Precision.HIGHEST


def _cparams(sem, vmem_mb):
    return pltpu.CompilerParams(dimension_semantics=sem, vmem_limit_bytes=vmem_mb * 1024 * 1024)


def _layer_norm(z, g, b):
    mu = jnp.mean(z, axis=-1, keepdims=True)
    zc = z - mu
    var = jnp.mean(zc * zc, axis=-1, keepdims=True)
    return zc * lax.rsqrt(var + LN_EPS) * g + b


def _ffn_ln_body(x_ref, wg_ref, wu_ref, wd_ref, g_ref, b_ref, o_ref, xb_ref, acc_ref, *, nf):
    j = pl.program_id(1)

    @pl.when(j == 0)
    def _():
        xb_ref[...] = x_ref[...].astype(BF16)
        acc_ref[...] = jnp.zeros_like(acc_ref)

    xb = xb_ref[...]
    gate = jnp.dot(xb, wg_ref[...], preferred_element_type=F32)
    up = jnp.dot(xb, wu_ref[...], preferred_element_type=F32)
    hmid = (gate * jax.nn.sigmoid(gate)) * up
    acc_ref[...] += jnp.dot(hmid.astype(BF16), wd_ref[...], preferred_element_type=F32)

    @pl.when(j == nf - 1)
    def _():
        z = ALPHA * x_ref[...] + 0.5 * acc_ref[...]
        o_ref[...] = _layer_norm(z, g_ref[...], b_ref[...])


def _ffn_ln(x, wg, wu, wd, g, b, *, tm=512, tf=512):
    m, d = x.shape
    dff = wg.shape[1]
    nf = dff // tf
    return pl.pallas_call(
        functools.partial(_ffn_ln_body, nf=nf),
        grid=(m // tm, nf),
        in_specs=[
            pl.BlockSpec((tm, d), lambda i, j: (i, 0)),
            pl.BlockSpec((d, tf), lambda i, j: (0, j)),
            pl.BlockSpec((d, tf), lambda i, j: (0, j)),
            pl.BlockSpec((tf, d), lambda i, j: (j, 0)),
            pl.BlockSpec((1, d), lambda i, j: (0, 0)),
            pl.BlockSpec((1, d), lambda i, j: (0, 0)),
        ],
        out_specs=pl.BlockSpec((tm, d), lambda i, j: (i, 0)),
        out_shape=jax.ShapeDtypeStruct((m, d), F32),
        scratch_shapes=[pltpu.VMEM((tm, d), BF16), pltpu.VMEM((tm, d), F32)],
        compiler_params=_cparams(("parallel", "arbitrary"), 48),
        name="ffn_ln",
    )(x, wg, wu, wd, g, b)


def _proj_body(x_ref, w_ref, o_ref, xb_ref):
    @pl.when(pl.program_id(1) == 0)
    def _():
        xb_ref[...] = x_ref[...].astype(BF16)

    o_ref[...] = jnp.dot(xb_ref[...], w_ref[...], preferred_element_type=F32)


def _proj(x, w, *, tm=512, tn=None, name="proj"):
    m, d = x.shape
    n = w.shape[1]
    tn = tn or n
    return pl.pallas_call(
        _proj_body,
        grid=(m // tm, n // tn),
        in_specs=[
            pl.BlockSpec((tm, d), lambda i, j: (i, 0)),
            pl.BlockSpec((d, tn), lambda i, j: (0, j)),
        ],
        out_specs=pl.BlockSpec((tm, tn), lambda i, j: (i, j)),
        out_shape=jax.ShapeDtypeStruct((m, n), F32),
        scratch_shapes=[pltpu.VMEM((tm, d), BF16)],
        compiler_params=_cparams(("parallel", "arbitrary"), 40),
        name=name,
    )(x, w)


def _rope_tables_body(pos_ref, invf_ref, rotm_ref, cos_ref, sin_ref):
    ang = pos_ref[...] * invf_ref[...]
    rot = rotm_ref[...] > 0.0
    cos_ref[...] = jnp.where(rot, jnp.cos(ang), 1.0)
    sin_ref[...] = jnp.where(rot, jnp.sin(ang), 0.0)


def _rope_tables(pos_f32, *, tm=1024):
    m = pos_f32.shape[0]
    half = ROT_DIM // 2
    inv_freq = jnp.power(ROPE_THETA, -jnp.arange(half, dtype=F32) * (2.0 / ROT_DIM))
    l64 = jnp.arange(LANES) % HEAD_DIM
    invf = jnp.where(l64 < ROT_DIM, inv_freq[l64 % half], 0.0).astype(F32)[None, :]
    rotm = (l64 < ROT_DIM).astype(F32)[None, :]
    return pl.pallas_call(
        _rope_tables_body,
        grid=(m // tm,),
        in_specs=[
            pl.BlockSpec((tm, 1), lambda i: (i, 0)),
            pl.BlockSpec((1, LANES), lambda i: (0, 0)),
            pl.BlockSpec((1, LANES), lambda i: (0, 0)),
        ],
        out_specs=[pl.BlockSpec((tm, LANES), lambda i: (i, 0))] * 2,
        out_shape=[jax.ShapeDtypeStruct((m, LANES), F32)] * 2,
        compiler_params=_cparams(("parallel",), 16),
        name="rope_tables",
    )(pos_f32, invf, rotm)


def _qkv_rope_body(x_ref, w_ref, cos_ref, sin_ref, o_ref, xb_ref):
    j = pl.program_id(1)

    @pl.when(j == 0)
    def _():
        xb_ref[...] = x_ref[...].astype(BF16)

    y = jnp.dot(xb_ref[...], w_ref[...], preferred_element_type=F32)

    @pl.when(j == 2)
    def _():
        o_ref[...] = y.astype(BF16)

    @pl.when(j < 2)
    def _():
        c = cos_ref[...]
        s = sin_ref[...]
        l64 = lax.broadcasted_iota(jnp.int32, (1, LANES), 1) & (HEAD_DIM - 1)
        lo = l64 < ROT_DIM // 2
        hi = l64 < ROT_DIM
        scale = jnp.where(j == 0, HEAD_DIM ** -0.5, 1.0).astype(F32)
        for t in range(ATTN_DIM // LANES):
            yt = y[:, t * LANES:(t + 1) * LANES]
            up = pltpu.roll(yt, LANES - ROT_DIM // 2, axis=1)
            dn = pltpu.roll(yt, ROT_DIM // 2, axis=1)
            rot = jnp.where(lo, -up, jnp.where(hi, dn, 0.0))
            o_ref[:, t * LANES:(t + 1) * LANES] = ((yt * c + rot * s) * scale).astype(BF16)


def _qkv_rope(x, w3, cos_t, sin_t, *, tm=512):
    m, d = x.shape
    n = w3.shape[2]
    return pl.pallas_call(
        _qkv_rope_body,
        grid=(m // tm, 3),
        in_specs=[
            pl.BlockSpec((tm, d), lambda i, j: (i, 0)),
            pl.BlockSpec((None, d, n), lambda i, j: (j, 0, 0)),
            pl.BlockSpec((tm, LANES), lambda i, j: (i, 0)),
            pl.BlockSpec((tm, LANES), lambda i, j: (i, 0)),
        ],
        out_specs=pl.BlockSpec((None, tm, n), lambda i, j: (j, i, 0)),
        out_shape=jax.ShapeDtypeStruct((3, m, n), BF16),
        scratch_shapes=[pltpu.VMEM((tm, d), BF16)],
        compiler_params=_cparams(("parallel", "arbitrary"), 32),
        name="qkv_rope",
    )(x, w3, cos_t, sin_t)


def _rwkv_prep_body(r_ref, k_ref, v_ref, l_ref, mur_ref, muk_ref, muv_ref, mul_ref,
                    w0_ref, w2_ref, a0_ref, a2_ref,
                    ro_ref, ko_ref, vo_ref, lw_ref, ao_ref, go_ref,
                    cr_ref, ck_ref, cv_ref, cl_ref, *, tiles_per_seq):
    i = pl.program_id(0)
    tm = r_ref.shape[0]
    row0 = lax.broadcasted_iota(jnp.int32, (tm, 1), 0) == 0

    @pl.when((i % tiles_per_seq) == 0)
    def _():
        for c_ref in (cr_ref, ck_ref, cv_ref, cl_ref):
            c_ref[...] = jnp.zeros_like(c_ref)

    def shift(z_ref, carry_ref, mu_ref):
        z = z_ref[...]
        prev = jnp.where(row0, carry_ref[0:1, :], pltpu.roll(z, 1, axis=0))
        carry_ref[0:1, :] = z[tm - 1:tm, :]
        return z + (prev - z) * mu_ref[...]

    ro_ref[...] = shift(r_ref, cr_ref, mur_ref)
    ko_ref[...] = shift(k_ref, ck_ref, muk_ref)
    vo_ref[...] = shift(v_ref, cv_ref, muv_ref)
    ls = shift(l_ref, cl_ref, mul_ref)
    w_lo = ls[:, 0:LANES]
    a_lo = ls[:, LANES:2 * LANES]
    go_ref[...] = ls[:, 2 * LANES:]
    w = w0_ref[...] + jnp.dot(jnp.tanh(w_lo), w2_ref[...], precision=HI, preferred_element_type=F32)
    x = -w
    softplus = jnp.maximum(x, 0.0) + jnp.log(1.0 + jnp.exp(-jnp.abs(x)))
    lw_ref[...] = -jnp.exp(-softplus - 0.5)
    a = a0_ref[...] + jnp.dot(a_lo, a2_ref[...], precision=HI, preferred_element_type=F32)
    ao_ref[...] = jax.nn.sigmoid(a)


def _rwkv_prep(rkv, lora, mu_r, mu_k, mu_v, mu_l, w0, w2p, a0, a2p, *, seq, tm=256):
    m = rkv.shape[0]
    n = RWKV_DIM
    row = lambda w: pl.BlockSpec((1, w), lambda i: (0, 0))
    return pl.pallas_call(
        functools.partial(_rwkv_prep_body, tiles_per_seq=seq // tm),
        grid=(m // tm,),
        in_specs=[
            pl.BlockSpec((tm, n), lambda i: (i, 0)),
            pl.BlockSpec((tm, n), lambda i: (i, 1)),
            pl.BlockSpec((tm, n), lambda i: (i, 2)),
            pl.BlockSpec((tm, LORA_PAD), lambda i: (i, 0)),
            row(n), row(n), row(n), row(LORA_PAD),
            row(n), pl.BlockSpec((LANES, n), lambda i: (0, 0)),
            row(n), pl.BlockSpec((LANES, n), lambda i: (0, 0)),
        ],
        out_specs=[pl.BlockSpec((tm, n), lambda i: (i, 0))] * 5
        + [pl.BlockSpec((tm, GATE_LORA), lambda i: (i, 0))],
        out_shape=[jax.ShapeDtypeStruct((m, n), F32)] * 5
        + [jax.ShapeDtypeStruct((m, GATE_LORA), F32)],
        scratch_shapes=[pltpu.VMEM((8, n), F32)] * 3 + [pltpu.VMEM((8, LORA_PAD), F32)],
        compiler_params=_cparams(("arbitrary",), 40),
        name="rwkv_prep",
    )(rkv, rkv, rkv, lora, mu_r, mu_k, mu_v, mu_l, w0, w2p, a0, a2p)


_NN = (((1,), (0,)), ((), ()))
_NT = (((1,), (1,)), ((), ()))
_TN = (((0,), (0,)), ((), ()))


def _split(a):
    hi = a.astype(BF16)
    return hi, (a - hi.astype(F32)).astype(BF16)


def _dot(a, b, dims=_NN):
    return lax.dot_general(a, b, dims, preferred_element_type=F32)


def _mm3(a, b, dims=_NN):
    ah, al = _split(a)
    bh, bl = _split(b)
    return _dot(ah, bh, dims) + (_dot(ah, bl, dims) + _dot(al, bh, dims))


def _cumsum_rows(tri, x):
    hi, lo = _split(x)
    lo2 = (x - hi.astype(F32) - lo.astype(F32)).astype(BF16)
    return _dot(tri, hi) + (_dot(tri, lo) + _dot(tri, lo2))


_MK_M0, _MK_M1, _MK_TL, _MK_BR, _MK_EYE, _MK_HEAD, _MK_LVL = 0, 1, 2, 3, 4, 5, 6


def _wkv_masks():
    C = CHUNK
    ri, ci = np.meshgrid(np.arange(2 * C), np.arange(2 * C), indexing="ij")
    ti, sj = ri % C, ci % C
    top, left = ri < C, ci < C
    masks = [
        (sj < ti) | (~top & (sj == ti)),
        (sj < ti) | (top & (sj == ti)),
        top & left,
        ~top & ~left,
        ri == ci,
        (ri < HEAD_DIM) == (ci < HEAD_DIM),
    ]
    s = 1
    while s < C:
        masks.append((ri // (2 * s) == ci // (2 * s)) & ((ri // s) % 2 == 1) & ((ci // s) % 2 == 0))
        s *= 2
    tri = np.tril(np.ones((C, C), np.float32))
    return jnp.asarray(np.stack(masks).astype(np.float32)), jnp.asarray(tri, dtype=BF16)


def _wkv_body(r_ref, lw_ref, k_ref, v_ref, a_ref, kk_ref, ka_ref, rk_ref, gg_ref, gb_ref,
              mk_ref, tri_ref, y_ref, g_ref, rp_ref, y0_ref, tm_ref, g0_ref):
    C = CHUNK
    nb, tc = r_ref.shape[0], r_ref.shape[1]
    nch = tc // C

    @pl.when(pl.program_id(1) == 0)
    def _():
        g_ref[...] = jnp.zeros_like(g_ref)

    lane = lax.broadcasted_iota(jnp.int32, (1, LANES), 1)
    h0 = lane < HEAD_DIM
    m0 = jnp.where(h0, 1.0, 0.0)
    m1 = 1.0 - m0
    zeros_c = jnp.zeros((C, LANES), BF16)
    tri = tri_ref[...]
    mk = lambda i: mk_ref[i]

    def segsum(x):
        s0 = jnp.sum(x * m0, axis=-1, keepdims=True)
        s1 = jnp.sum(x * m1, axis=-1, keepdims=True)
        return jnp.where(h0, s0, s1)

    kk_p, ka_p, rk_p = kk_ref[...], ka_ref[...], rk_ref[...]
    b16 = lambda x: x.astype(BF16)
    cat0 = lambda *xs: jnp.concatenate(xs, 0)
    cat1 = lambda *xs: jnp.concatenate(xs, 1)

    each = lambda f, *ls: [f(*xs) for xs in zip(*ls)]
    ch_r, ch_lw, ch_k, ch_v, ch_a, ch_b = [], [], [], [], [], []
    for b in range(nb):
        r, ks, v, a = r_ref[b], k_ref[b], v_ref[b], a_ref[b]
        kk = ks * kk_p
        kk = kk / jnp.maximum(jnp.sqrt(segsum(kk * kk)), 1e-12)
        kmod = ks * (1.0 + (a - 1.0) * ka_p)
        y_ref[b] = segsum(r * kmod * rk_p) * v
        bv = kk * a
        for c in range(nch):
            rows = slice(c * C, (c + 1) * C)
            ch_r.append(r[rows]); ch_lw.append(lw_ref[b, rows, :]); ch_k.append(kmod[rows])
            ch_v.append(v[rows]); ch_a.append(-kk[rows]); ch_b.append(bv[rows])

    cs = each(lambda lw: _cumsum_rows(tri, lw), ch_lw)
    cl = each(lambda s: s[C - 1:C, :], cs)
    At = each(lambda av, s, lw: av * jnp.exp(s - lw), ch_a, cs, ch_lw)
    Rt = each(lambda r, s: r * jnp.exp(s), ch_r, cs)
    pinv = each(lambda s: jnp.exp(-s), cs)
    Bt = each(lambda bv, p: b16(bv * p), ch_b, pinv)
    Kt = each(lambda k, p: b16(k * p), ch_k, pinv)
    pend = each(lambda l, s: jnp.exp(l - s), cl, cs)
    BKh = each(lambda bv, k, p: b16(cat0(bv * p, k * p)), ch_b, ch_k, pend)
    At0 = each(lambda x: b16(x * m0), At)
    At1 = each(lambda x: b16(x * m1), At)

    M0 = each(lambda a0, rt, bt, kt: _dot(cat0(a0, b16(rt * m0)), cat0(bt, kt), _NT), At0, Rt, Bt, Kt)
    M1 = each(lambda a1, rt, bt, kt: _dot(cat0(b16(rt * m1), a1), cat0(kt, bt), _NT), At1, Rt, Bt, Kt)
    M0 = each(lambda x: jnp.where(mk(_MK_M0) > 0.5, x, 0.0), M0)
    M1 = each(lambda x: jnp.where(mk(_MK_M1) > 0.5, x, 0.0), M1)
    M0b, M1b = each(b16, M0), each(b16, M1)

    Lbd = each(lambda x0, x1: x0 * mk(_MK_TL) + x1 * mk(_MK_BR), M0, M1)
    X = each(lambda l: mk(_MK_EYE) + l * mk(_MK_LVL), Lbd)
    for lv in range(1, C.bit_length() - 1):
        Xb = each(b16, X)
        XC = each(lambda xb, l: b16(_dot(xb, b16(l * mk(_MK_LVL + lv)))), Xb, Lbd)
        X = each(lambda x, xc, xb: x + _dot(xc, xb), X, XC, Xb)
    Xb = each(b16, X)

    v0 = each(lambda v: b16(v * m0), ch_v)
    v1 = each(lambda v: b16(v * m1), ch_v)
    W = each(lambda a, b_, x0, x1: _dot(cat1(a[:C], b_[C:]), cat0(zeros_c, x0, x1, zeros_c)), M0b, M1b, v0, v1)
    res = each(lambda xb, a0, a1, w: _dot(xb, cat0(cat1(a0, b16(w * m0)), cat1(a1, b16(w * m1)))),
               Xb, At0, At1, W)
    Ap = each(lambda x: x[:C, :LANES] + x[C:, :LANES], res)
    XW = each(lambda x: x[:C, LANES:] + x[C:, LANES:], res)

    def y_rhs(ap, xw, x0, x1):
        return cat0(cat1(b16(ap * m0), b16(xw * m0)), cat1(zeros_c, x0), cat1(zeros_c, x1),
                    cat1(b16(ap * m1), b16(xw * m1)))

    res_y = each(lambda a, b_, ap, xw, x0, x1: _dot(cat1(a[C:], b_[:C]), y_rhs(ap, xw, x0, x1)),
                 M0b, M1b, Ap, XW, v0, v1)
    res_t = each(lambda ap, xw, v, bk: _dot(cat0(cat1(b16(ap), b16(xw)), cat1(zeros_c, b16(v))), bk, _TN),
                 Ap, XW, ch_v, BKh)
    for i in range(nb * nch):
        b, c = divmod(i, nch)
        rows = slice(c * C, (c + 1) * C)
        rp_ref[b, rows, :] = Rt[i] + res_y[i][:, :LANES]
        y0_ref[b, rows, :] = res_y[i][:, LANES:]
        tm_ref[i] = (res_t[i][:LANES] + mk(_MK_EYE) * jnp.exp(cl[i])) * mk(_MK_HEAD)
        g0_ref[i] = res_t[i][LANES:] * mk(_MK_HEAD)

    Gs = [g_ref[b] for b in range(nb)]
    for c in range(nch):
        rows = slice(c * C, (c + 1) * C)
        for b in range(nb):
            y0_ref[b, rows, :] = _dot(b16(rp_ref[b, rows, :]), b16(Gs[b]), _NT) + y0_ref[b, rows, :]
        Gs = [_mm3(Gs[b], tm_ref[b * nch + c]) + g0_ref[b * nch + c] for b in range(nb)]
    for b in range(nb):
        g_ref[b] = Gs[b]

    gn_g, gn_b = gg_ref[...], gb_ref[...]
    for b in range(nb):
        y = y0_ref[b]
        yc = y - segsum(y) * (1.0 / HEAD_DIM)
        var = segsum(yc * yc) * (1.0 / HEAD_DIM)
        y_ref[b] = y_ref[b] + (yc * lax.rsqrt(var + GN_EPS) * gn_g + gn_b)


def _wkv(r, lw, k, v, a, k_k, k_a, r_k, gn_g, gn_b, *, batch, seq, tc=256):
    n = RWKV_DIM
    shp = (batch, seq, n)
    tc = min(tc, seq)
    nch = tc // CHUNK
    masks, tri = _wkv_masks()
    seq_spec = pl.BlockSpec((batch, tc, LANES), lambda p, i: (0, i, p))
    par_spec = pl.BlockSpec((1, LANES), lambda p, i: (0, p))
    out = pl.pallas_call(
        _wkv_body,
        grid=(n // LANES, seq // tc),
        in_specs=[seq_spec] * 5 + [par_spec] * 5 + [
            pl.BlockSpec(masks.shape, lambda p, i: (0, 0, 0)),
            pl.BlockSpec(tri.shape, lambda p, i: (0, 0)),
        ],
        out_specs=seq_spec,
        out_shape=jax.ShapeDtypeStruct(shp, F32),
        scratch_shapes=[
            pltpu.VMEM((batch, LANES, LANES), F32),
            pltpu.VMEM((batch, tc, LANES), F32),
            pltpu.VMEM((batch, tc, LANES), F32),
            pltpu.VMEM((batch * nch, LANES, LANES), F32),
            pltpu.VMEM((batch * nch, LANES, LANES), F32),
        ],
        compiler_params=_cparams(("parallel", "arbitrary"), 32),
        name="wkv",
    )(r.reshape(shp), lw.reshape(shp), k.reshape(shp), v.reshape(shp), a.reshape(shp),
      k_k, k_a, r_k, gn_g, gn_b, masks, tri)
    return out.reshape(batch * seq, n)


def _dil_attn_body(q_ref, kc_ref, vc_ref, kp_ref, vp_ref, o_ref, lse_ref):
    blk = ATTN_BLOCK
    tt = q_ref.shape[0]
    first_lo = jnp.where(pl.program_id(3) == 0, blk, 0)

    lane = lax.broadcasted_iota(jnp.int32, (1, LANES), 1)
    h0 = lane < HEAD_DIM
    qi = lax.broadcasted_iota(jnp.int32, (blk, 2 * blk), 0)
    kj = lax.broadcasted_iota(jnp.int32, (blk, 2 * blk), 1)
    band = (kj >= qi) & (kj <= qi + blk)

    for j in range(tt // blk):
        q = q_ref[j * blk:(j + 1) * blk, :]
        if j == 0:
            keys = jnp.concatenate([kp_ref[...], kc_ref[0:blk, :]], 0)
            vals = jnp.concatenate([vp_ref[...], vc_ref[0:blk, :]], 0)
            valid = band & (kj >= first_lo)
        else:
            keys = kc_ref[(j - 1) * blk:(j + 1) * blk, :]
            vals = vc_ref[(j - 1) * blk:(j + 1) * blk, :]
            valid = band
        zero = jnp.zeros_like(q)
        ps, dens, ms = [], [], []
        for hm in (h0, jnp.logical_not(h0)):
            s = lax.dot_general(jnp.where(hm, q, zero), keys, _NT, preferred_element_type=F32)
            s = jnp.where(valid, s, NEG_BIG)
            m = jnp.max(s, axis=-1, keepdims=True)
            p = jnp.exp(s - m)
            dens.append(jnp.sum(p, axis=-1, keepdims=True))
            ms.append(m)
            ps.append(p.astype(BF16))
        zv = jnp.zeros_like(vals)
        vbd = jnp.concatenate([jnp.where(h0, vals, zv), jnp.where(h0, zv, vals)], 0)
        o = jnp.dot(jnp.concatenate(ps, 1), vbd, preferred_element_type=F32)
        den = jnp.where(h0, dens[0], dens[1])
        o_ref[j * blk:(j + 1) * blk, :] = o / den
        lse_ref[j * blk:(j + 1) * blk, :] = jnp.where(h0, ms[0], ms[1]) + jnp.log(den)


def _dil_attn(qkv, *, batch, seq, dil, tt=512):
    ld = seq // dil
    tt = min(tt, ld)
    npair = ATTN_DIM // LANES
    x = qkv.reshape(3, batch, ld, dil * ATTN_DIM)
    nb = tt // ATTN_BLOCK

    def cur(which):
        return pl.BlockSpec((None, None, tt, LANES), lambda b, r, p, i: (which, b, i, r * npair + p))

    def prev(which):
        return pl.BlockSpec((None, None, ATTN_BLOCK, LANES),
                            lambda b, r, p, i: (which, b, jnp.maximum(i * nb - 1, 0), r * npair + p))

    out_spec = pl.BlockSpec((None, tt, LANES), lambda b, r, p, i: (b, i, r * npair + p))
    o, lse = pl.pallas_call(
        _dil_attn_body,
        grid=(batch, dil, npair, ld // tt),
        in_specs=[cur(0), cur(1), cur(2), prev(1), prev(2)],
        out_specs=[out_spec, out_spec],
        out_shape=[jax.ShapeDtypeStruct((batch, ld, dil * ATTN_DIM), F32)] * 2,
        compiler_params=_cparams(("parallel", "parallel", "parallel", "arbitrary"), 32),
        name=f"dil_attn_d{dil}",
    )(x, x, x, x, x)
    return o.reshape(batch * seq, ATTN_DIM), lse.reshape(batch * seq, ATTN_DIM)


def _out_ln_body(yw_ref, gl_ref, o1_ref, o2_ref, o3_ref, l1_ref, l2_ref, l3_ref, h_ref,
                 g2_ref, wr_ref, wa_ref, g_ref, b_ref, out_ref):
    gate = jnp.dot(jax.nn.sigmoid(gl_ref[...]).astype(BF16), g2_ref[...], preferred_element_type=F32)
    yr = yw_ref[...] * gate
    l1, l2, l3 = l1_ref[...], l2_ref[...], l3_ref[...]
    mx = jnp.maximum(jnp.maximum(l1, l2), l3)
    e1, e2, e3 = jnp.exp(l1 - mx), jnp.exp(l2 - mx), jnp.exp(l3 - mx)
    den = e1 + e2 + e3
    ya = (e1 / den) * o1_ref[...] + (e2 / den) * o2_ref[...] + (e3 / den) * o3_ref[...]
    mix = (jnp.dot(yr.astype(BF16), wr_ref[...], preferred_element_type=F32)
           + jnp.dot(ya.astype(BF16), wa_ref[...], preferred_element_type=F32))
    z = ALPHA * h_ref[...] + mix
    out_ref[...] = _layer_norm(z, g_ref[...], b_ref[...])


def _out_ln(yw, glo, os_, lses, h, g2, wo_r, wo_a, g, b, *, tm=256):
    m, d = h.shape
    rows = lambda w: pl.BlockSpec((tm, w), lambda i: (i, 0))
    full = lambda a: pl.BlockSpec(a.shape, lambda i: (0, 0))
    return pl.pallas_call(
        _out_ln_body,
        grid=(m // tm,),
        in_specs=[rows(RWKV_DIM), rows(GATE_LORA)] + [rows(ATTN_DIM)] * 6 + [rows(d),
                  full(g2), full(wo_r), full(wo_a), full(g), full(b)],
        out_specs=rows(d),
        out_shape=jax.ShapeDtypeStruct((m, d), F32),
        compiler_params=_cparams(("parallel",), 48),
        name="out_ln",
    )(yw, glo, *os_, *lses, h, g2, wo_r, wo_a, g, b)


def _pad_rows(w, rows):
    return jnp.pad(w, ((0, rows - w.shape[0]), (0, 0)))


def _pad_cols(w, cols):
    return jnp.pad(w, ((0, 0), (0, cols - w.shape[1])))


def kernel(x, positions, ffn1_w_gate, ffn1_w_up, ffn1_w_down, ln1_g, ln1_b, w_in, mu_r, mu_k, mu_v, mu_w, mu_a, mu_g, w0, w2, a0, a2, g2, k_k, k_a, r_k, gn_g, gn_b, w_out, ln2_g, ln2_b, ffn2_w_gate, ffn2_w_up, ffn2_w_down, ln3_g, ln3_b):
    batch, seq, d = x.shape
    m = batch * seq
    h = x.reshape(m, d)
    cos_t, sin_t = _rope_tables(positions.astype(F32).reshape(m, 1))
    c = RWKV_DIM
    for l in range(DEPTH):
        row = lambda p: p[l].reshape(1, -1)
        h = _ffn_ln(h, ffn1_w_gate[l].astype(BF16), ffn1_w_up[l].astype(BF16),
                    ffn1_w_down[l].astype(BF16), row(ln1_g), row(ln1_b))

        wi = w_in[l]
        o_lo = 3 * c
        o_q = o_lo + DECAY_LORA + ICLR_LORA + GATE_LORA
        w_rkv = wi[:, :o_lo].astype(BF16)
        w_lora = jnp.concatenate([
            _pad_cols(wi[:, o_lo:o_lo + DECAY_LORA], LANES),
            _pad_cols(wi[:, o_lo + DECAY_LORA:o_lo + DECAY_LORA + ICLR_LORA], LANES),
            wi[:, o_lo + DECAY_LORA + ICLR_LORA:o_q]], axis=1).astype(BF16)
        w_qkv = wi[:, o_q:].reshape(d, 3, ATTN_DIM).transpose(1, 0, 2).astype(BF16)
        mu_l = jnp.concatenate([_pad_cols(row(mu_w), LANES), _pad_cols(row(mu_a), LANES), row(mu_g)], axis=1)

        rkv = _proj(h, w_rkv, tn=c, name="proj_rkv")
        lora = _proj(h, w_lora, name="proj_lora")
        qkv = _qkv_rope(h, w_qkv, cos_t, sin_t)

        r_s, k_s, v_s, lw, a_s, glo = _rwkv_prep(
            rkv, lora, row(mu_r), row(mu_k), row(mu_v), mu_l,
            row(w0), _pad_rows(w2[l], LANES), row(a0), _pad_rows(a2[l], LANES), seq=seq)
        yw = _wkv(r_s, lw, k_s, v_s, a_s, row(k_k), row(k_a), row(r_k), row(gn_g), row(gn_b),
                  batch=batch, seq=seq)

        branches = [_dil_attn(qkv, batch=batch, seq=seq, dil=dil) for dil in DILATIONS]
        os_ = [o for o, _ in branches]
        lses = [s for _, s in branches]

        wo = w_out[l].astype(BF16)
        h = _out_ln(yw, glo, os_, lses, h, g2[l].astype(BF16), wo[:c], wo[c:], row(ln2_g), row(ln2_b))

        h = _ffn_ln(h, ffn2_w_gate[l].astype(BF16), ffn2_w_up[l].astype(BF16),
                    ffn2_w_down[l].astype(BF16), row(ln3_g), row(ln3_b))
    return h.reshape(batch, seq, d)
```

```python
import functools

import numpy as np

import jax
import jax.numpy as jnp
from jax import lax
from jax.experimental import pallas as pl
from jax.experimental.pallas import tpu as pltpu

F32 = jnp.float32
BF16 = jnp.bfloat16

D_MODEL = 2048
HEAD_DIM = 64
ATTN_HEADS = 12
RWKV_HEADS = 20
ATTN_DIM = ATTN_HEADS * HEAD_DIM
RWKV_DIM = RWKV_HEADS * HEAD_DIM
DECAY_LORA = 96
ICLR_LORA = 96
GATE_LORA = 256
DILATIONS = (1, 4, 16)
ATTN_BLOCK = 128
ROPE_THETA = 500000.0
ROT_DIM = HEAD_DIM // 4
D_FF = 5632
DEPTH = 1
ALPHA = (2 * DEPTH) ** 0.25
LN_EPS = 1e-5
GN_EPS = 64e-5

LANES = 128
LORA_PAD = 512
CHUNK = 64
NEG_BIG = -1e30
HI = lax.Precision.HIGHEST


def _cparams(sem, vmem_mb):
    return pltpu.CompilerParams(dimension_semantics=sem, vmem_limit_bytes=vmem_mb * 1024 * 1024)


def _layer_norm(z, g, b):
    mu = jnp.mean(z, axis=-1, keepdims=True)
    zc = z - mu
    var = jnp.mean(zc * zc, axis=-1, keepdims=True)
    return zc * lax.rsqrt(var + LN_EPS) * g + b


def _ffn_ln_body(x_ref, wg_ref, wu_ref, wd_ref, g_ref, b_ref, o_ref, xb_ref, acc_ref, *, nf):
    j = pl.program_id(1)

    @pl.when(j == 0)
    def _():
        xb_ref[...] = x_ref[...].astype(BF16)
        acc_ref[...] = jnp.zeros_like(acc_ref)

    xb = xb_ref[...]
    gate = jnp.dot(xb, wg_ref[...], preferred_element_type=F32)
    up = jnp.dot(xb, wu_ref[...], preferred_element_type=F32)
    hmid = (gate * jax.nn.sigmoid(gate)) * up
    acc_ref[...] += jnp.dot(hmid.astype(BF16), wd_ref[...], preferred_element_type=F32)

    @pl.when(j == nf - 1)
    def _():
        z = ALPHA * x_ref[...] + 0.5 * acc_ref[...]
        o_ref[...] = _layer_norm(z, g_ref[...], b_ref[...])


def _ffn_ln(x, wg, wu, wd, g, b, *, tm=512, tf=512):
    m, d = x.shape
    dff = wg.shape[1]
    nf = dff // tf
    return pl.pallas_call(
        functools.partial(_ffn_ln_body, nf=nf),
        grid=(m // tm, nf),
        in_specs=[
            pl.BlockSpec((tm, d), lambda i, j: (i, 0)),
            pl.BlockSpec((d, tf), lambda i, j: (0, j)),
            pl.BlockSpec((d, tf), lambda i, j: (0, j)),
            pl.BlockSpec((tf, d), lambda i, j: (j, 0)),
            pl.BlockSpec((1, d), lambda i, j: (0, 0)),
            pl.BlockSpec((1, d), lambda i, j: (0, 0)),
        ],
        out_specs=pl.BlockSpec((tm, d), lambda i, j: (i, 0)),
        out_shape=jax.ShapeDtypeStruct((m, d), F32),
        scratch_shapes=[pltpu.VMEM((tm, d), BF16), pltpu.VMEM((tm, d), F32)],
        compiler_params=_cparams(("parallel", "arbitrary"), 48),
        name="ffn_ln",
    )(x, wg, wu, wd, g, b)


def _rope_tables_body(pos_ref, invf_ref, rotm_ref, cos_ref, sin_ref):
    ang = pos_ref[...] * invf_ref[...]
    rot = rotm_ref[...] > 0.0
    cos_ref[...] = jnp.where(rot, jnp.cos(ang), 1.0)
    sin_ref[...] = jnp.where(rot, jnp.sin(ang), 0.0)


def _rope_tables(pos_f32, *, tm=1024):
    m = pos_f32.shape[0]
    half = ROT_DIM // 2
    inv_freq = jnp.power(ROPE_THETA, -jnp.arange(half, dtype=F32) * (2.0 / ROT_DIM))
    l64 = jnp.arange(LANES) % HEAD_DIM
    invf = jnp.where(l64 < ROT_DIM, inv_freq[l64 % half], 0.0).astype(F32)[None, :]
    rotm = (l64 < ROT_DIM).astype(F32)[None, :]
    return pl.pallas_call(
        _rope_tables_body,
        grid=(m // tm,),
        in_specs=[
            pl.BlockSpec((tm, 1), lambda i: (i, 0)),
            pl.BlockSpec((1, LANES), lambda i: (0, 0)),
            pl.BlockSpec((1, LANES), lambda i: (0, 0)),
        ],
        out_specs=[pl.BlockSpec((tm, LANES), lambda i: (i, 0))] * 2,
        out_shape=[jax.ShapeDtypeStruct((m, LANES), F32)] * 2,
        compiler_params=_cparams(("parallel",), 16),
        name="rope_tables",
    )(pos_f32, invf, rotm)


def _qkv_rope_body(x_ref, w_ref, cos_ref, sin_ref, o_ref, xb_ref):
    j = pl.program_id(1)

    @pl.when(j == 0)
    def _():
        xb_ref[...] = x_ref[...].astype(BF16)

    y = jnp.dot(xb_ref[...], w_ref[...], preferred_element_type=F32)

    @pl.when(j == 2)
    def _():
        o_ref[...] = y

    @pl.when(j < 2)
    def _():
        c = cos_ref[...]
        s = sin_ref[...]
        l64 = lax.broadcasted_iota(jnp.int32, (1, LANES), 1) & (HEAD_DIM - 1)
        lo = l64 < ROT_DIM // 2
        hi = l64 < ROT_DIM
        scale = jnp.where(j == 0, HEAD_DIM ** -0.5, 1.0).astype(F32)
        for t in range(ATTN_DIM // LANES):
            yt = y[:, t * LANES:(t + 1) * LANES]
            up = pltpu.roll(yt, LANES - ROT_DIM // 2, axis=1)
            dn = pltpu.roll(yt, ROT_DIM // 2, axis=1)
            rot = jnp.where(lo, -up, jnp.where(hi, dn, 0.0))
            o_ref[:, t * LANES:(t + 1) * LANES] = (yt * c + rot * s) * scale


def _qkv_rope(x, w3, cos_t, sin_t, *, tm=512):
    m, d = x.shape
    n = w3.shape[2]
    return pl.pallas_call(
        _qkv_rope_body,
        grid=(m // tm, 3),
        in_specs=[
            pl.BlockSpec((tm, d), lambda i, j: (i, 0)),
            pl.BlockSpec((None, d, n), lambda i, j: (j, 0, 0)),
            pl.BlockSpec((tm, LANES), lambda i, j: (i, 0)),
            pl.BlockSpec((tm, LANES), lambda i, j: (i, 0)),
        ],
        out_specs=pl.BlockSpec((None, tm, n), lambda i, j: (j, i, 0)),
        out_shape=jax.ShapeDtypeStruct((3, m, n), F32),
        scratch_shapes=[pltpu.VMEM((tm, d), BF16)],
        compiler_params=_cparams(("parallel", "arbitrary"), 32),
        name="qkv_rope",
    )(x, w3, cos_t, sin_t)


def _rwkv_proj_body(x_ref, w_ref, wl_ref, mu_ref, mul_ref, w0_ref, w2_ref, a0_ref, a2_ref,
                    rkv_ref, lw_ref, ao_ref, go_ref, xb_ref, carry_ref, cl_ref, *, tiles_per_seq):
    i, j = pl.program_id(0), pl.program_id(1)
    tm = x_ref.shape[0]
    row0 = lax.broadcasted_iota(jnp.int32, (tm, 1), 0) == 0

    @pl.when(j == 0)
    def _():
        xb_ref[...] = x_ref[...].astype(BF16)

    @pl.when(jnp.logical_and(j == 0, (i % tiles_per_seq) == 0))
    def _():
        carry_ref[...] = jnp.zeros_like(carry_ref)
        cl_ref[...] = jnp.zeros_like(cl_ref)

    def shift(z, carry, mu):
        prev = jnp.where(row0, carry[0:1, :], pltpu.roll(z, 1, axis=0))
        carry[0:1, :] = z[tm - 1:tm, :]
        return z + (prev - z) * mu

    @pl.when(j < 3)
    def _():
        z = jnp.dot(xb_ref[...], w_ref[...], preferred_element_type=F32)
        rkv_ref[...] = shift(z, carry_ref.at[j], mu_ref[...])

    @pl.when(j == 3)
    def _():
        z = jnp.dot(xb_ref[...], wl_ref[...], preferred_element_type=F32)
        ls = shift(z, cl_ref, mul_ref[...])
        w_lo = ls[:, 0:LANES]
        a_lo = ls[:, LANES:2 * LANES]
        go_ref[...] = ls[:, 2 * LANES:]
        w = w0_ref[...] + jnp.dot(jnp.tanh(w_lo), w2_ref[...], precision=HI, preferred_element_type=F32)
        x = -w
        softplus = jnp.maximum(x, 0.0) + jnp.log(1.0 + jnp.exp(-jnp.abs(x)))
        lw_ref[...] = -jnp.exp(-softplus - 0.5)
        a = a0_ref[...] + jnp.dot(a_lo, a2_ref[...], precision=HI, preferred_element_type=F32)
        ao_ref[...] = jax.nn.sigmoid(a)


def _rwkv_proj(x, w_rkv, w_lora, mu_rkv, mu_l, w0, w2p, a0, a2p, *, seq, tm=512):
    m, d = x.shape
    n = RWKV_DIM
    row = lambda w: pl.BlockSpec((1, w), lambda i, j: (0, 0))
    col = lambda j: jnp.minimum(j, 2)
    return pl.pallas_call(
        functools.partial(_rwkv_proj_body, tiles_per_seq=seq // tm),
        grid=(m // tm, 4),
        in_specs=[
            pl.BlockSpec((tm, d), lambda i, j: (i, 0)),
            pl.BlockSpec((d, n), lambda i, j: (0, col(j))),
            pl.BlockSpec((d, LORA_PAD), lambda i, j: (0, 0)),
            pl.BlockSpec((None, 1, n), lambda i, j: (col(j), 0, 0)),
            row(LORA_PAD),
            row(n), pl.BlockSpec((LANES, n), lambda i, j: (0, 0)),
            row(n), pl.BlockSpec((LANES, n), lambda i, j: (0, 0)),
        ],
        out_specs=[pl.BlockSpec((None, tm, n), lambda i, j: (col(j), i, 0))]
        + [pl.BlockSpec((tm, n), lambda i, j: (i, 0))] * 2
        + [pl.BlockSpec((tm, GATE_LORA), lambda i, j: (i, 0))],
        out_shape=[jax.ShapeDtypeStruct((3, m, n), F32)]
        + [jax.ShapeDtypeStruct((m, n), F32)] * 2
        + [jax.ShapeDtypeStruct((m, GATE_LORA), F32)],
        scratch_shapes=[pltpu.VMEM((tm, d), BF16), pltpu.VMEM((3, 8, n), F32), pltpu.VMEM((8, LORA_PAD), F32)],
        compiler_params=_cparams(("arbitrary", "arbitrary"), 56),
        name="rwkv_proj",
    )(x, w_rkv, w_lora, mu_rkv, mu_l, w0, w2p, a0, a2p)


_NN = (((1,), (0,)), ((), ()))
_NT = (((1,), (1,)), ((), ()))
_TN = (((0,), (0,)), ((), ()))


def _split(a):
    hi = a.astype(BF16)
    return hi, (a - hi.astype(F32)).astype(BF16)


def _dot(a, b, dims=_NN):
    return lax.dot_general(a, b, dims, preferred_element_type=F32)


def _mm3(a, b, dims=_NN):
    ah, al = _split(a)
    bh, bl = _split(b)
    return _dot(ah, bh, dims) + (_dot(ah, bl, dims) + _dot(al, bh, dims))


def _cumsum_rows(tri, x):
    hi, lo = _split(x)
    lo2 = (x - hi.astype(F32) - lo.astype(F32)).astype(BF16)
    return _dot(tri, hi) + (_dot(tri, lo) + _dot(tri, lo2))


_MK_M0, _MK_M1, _MK_TL, _MK_BR, _MK_EYE, _MK_HEAD, _MK_LVL = 0, 1, 2, 3, 4, 5, 6


def _wkv_masks():
    C = CHUNK
    ri, ci = np.meshgrid(np.arange(2 * C), np.arange(2 * C), indexing="ij")
    ti, sj = ri % C, ci % C
    top, left = ri < C, ci < C
    masks = [
        (sj < ti) | (~top & (sj == ti)),
        (sj < ti) | (top & (sj == ti)),
        top & left,
        ~top & ~left,
        ri == ci,
        (ri < HEAD_DIM) == (ci < HEAD_DIM),
    ]
    s = 1
    while s < C:
        masks.append((ri // (2 * s) == ci // (2 * s)) & ((ri // s) % 2 == 1) & ((ci // s) % 2 == 0))
        s *= 2
    tri = np.tril(np.ones((C, C), np.float32))
    return jnp.asarray(np.stack(masks).astype(np.float32)), jnp.asarray(tri, dtype=BF16)


def _wkv_body(r_ref, lw_ref, k_ref, v_ref, a_ref, kk_ref, ka_ref, rk_ref, gg_ref, gb_ref,
              mk_ref, tri_ref, y_ref, g_ref, rp_ref, y0_ref, tm_ref, g0_ref):
    C = CHUNK
    nb, tc = r_ref.shape[0], r_ref.shape[1]
    nch = tc // C

    @pl.when(pl.program_id(1) == 0)
    def _():
        g_ref[...] = jnp.zeros_like(g_ref)

    lane = lax.broadcasted_iota(jnp.int32, (1, LANES), 1)
    h0 = lane < HEAD_DIM
    m0 = jnp.where(h0, 1.0, 0.0)
    m1 = 1.0 - m0
    zeros_c = jnp.zeros((C, LANES), BF16)
    tri = tri_ref[...]
    mk = lambda i: mk_ref[i]

    def segsum(x):
        s0 = jnp.sum(x * m0, axis=-1, keepdims=True)
        s1 = jnp.sum(x * m1, axis=-1, keepdims=True)
        return jnp.where(h0, s0, s1)

    kk_p, ka_p, rk_p = kk_ref[...], ka_ref[...], rk_ref[...]
    b16 = lambda x: x.astype(BF16)
    cat0 = lambda *xs: jnp.concatenate(xs, 0)
    cat1 = lambda *xs: jnp.concatenate(xs, 1)

    each = lambda f, *ls: [f(*xs) for xs in zip(*ls)]
    ch_r, ch_lw, ch_k, ch_v, ch_a, ch_b = [], [], [], [], [], []
    for b in range(nb):
        r, ks, v, a = r_ref[b], k_ref[b], v_ref[b], a_ref[b]
        kk = ks * kk_p
        kk = kk / jnp.maximum(jnp.sqrt(segsum(kk * kk)), 1e-12)
        kmod = ks * (1.0 + (a - 1.0) * ka_p)
        y_ref[b] = segsum(r * kmod * rk_p) * v
        bv = kk * a
        for c in range(nch):
            rows = slice(c * C, (c + 1) * C)
            ch_r.append(r[rows]); ch_lw.append(lw_ref[b, rows, :]); ch_k.append(kmod[rows])
            ch_v.append(v[rows]); ch_a.append(-kk[rows]); ch_b.append(bv[rows])

    cs = each(lambda lw: _cumsum_rows(tri, lw), ch_lw)
    cl = each(lambda s: s[C - 1:C, :], cs)
    At = each(lambda av, s, lw: av * jnp.exp(s - lw), ch_a, cs, ch_lw)
    Rt = each(lambda r, s: r * jnp.exp(s), ch_r, cs)
    pinv = each(lambda s: jnp.exp(-s), cs)
    Bt = each(lambda bv, p: b16(bv * p), ch_b, pinv)
    Kt = each(lambda k, p: b16(k * p), ch_k, pinv)
    pend = each(lambda l, s: jnp.exp(l - s), cl, cs)
    BKh = each(lambda bv, k, p: b16(cat0(bv * p, k * p)), ch_b, ch_k, pend)
    At0 = each(lambda x: b16(x * m0), At)
    At1 = each(lambda x: b16(x * m1), At)

    M0 = each(lambda a0, rt, bt, kt: _dot(cat0(a0, b16(rt * m0)), cat0(bt, kt), _NT), At0, Rt, Bt, Kt)
    M1 = each(lambda a1, rt, bt, kt: _dot(cat0(b16(rt * m1), a1), cat0(kt, bt), _NT), At1, Rt, Bt, Kt)
    M0 = each(lambda x: jnp.where(mk(_MK_M0) > 0.5, x, 0.0), M0)
    M1 = each(lambda x: jnp.where(mk(_MK_M1) > 0.5, x, 0.0), M1)
    M0b, M1b = each(b16, M0), each(b16, M1)

    Lbd = each(lambda x0, x1: x0 * mk(_MK_TL) + x1 * mk(_MK_BR), M0, M1)
    X = each(lambda l: mk(_MK_EYE) + l * mk(_MK_LVL), Lbd)
    for lv in range(1, C.bit_length() - 1):
        Xb = each(b16, X)
        XC = each(lambda xb, l: b16(_dot(xb, b16(l * mk(_MK_LVL + lv)))), Xb, Lbd)
        X = each(lambda x, xc, xb: x + _dot(xc, xb), X, XC, Xb)
    Xb = each(b16, X)

    v0 = each(lambda v: b16(v * m0), ch_v)
    v1 = each(lambda v: b16(v * m1), ch_v)
    W = each(lambda a, b_, x0, x1: _dot(cat1(a[:C], b_[C:]), cat0(zeros_c, x0, x1, zeros_c)), M0b, M1b, v0, v1)
    res = each(lambda xb, a0, a1, w: _dot(xb, cat0(cat1(a0, b16(w * m0)), cat1(a1, b16(w * m1)))),
               Xb, At0, At1, W)
    Ap = each(lambda x: x[:C, :LANES] + x[C:, :LANES], res)
    XW = each(lambda x: x[:C, LANES:] + x[C:, LANES:], res)

    def y_rhs(ap, xw, x0, x1):
        return cat0(cat1(b16(ap * m0), b16(xw * m0)), cat1(zeros_c, x0), cat1(zeros_c, x1),
                    cat1(b16(ap * m1), b16(xw * m1)))

    res_y = each(lambda a, b_, ap, xw, x0, x1: _dot(cat1(a[C:], b_[:C]), y_rhs(ap, xw, x0, x1)),
                 M0b, M1b, Ap, XW, v0, v1)
    res_t = each(lambda ap, xw, v, bk: _dot(cat0(cat1(b16(ap), b16(xw)), cat1(zeros_c, b16(v))), bk, _TN),
                 Ap, XW, ch_v, BKh)
    for i in range(nb * nch):
        b, c = divmod(i, nch)
        rows = slice(c * C, (c + 1) * C)
        rp_ref[b, rows, :] = Rt[i] + res_y[i][:, :LANES]
        y0_ref[b, rows, :] = res_y[i][:, LANES:]
        tm_ref[i] = (res_t[i][:LANES] + mk(_MK_EYE) * jnp.exp(cl[i])) * mk(_MK_HEAD)
        g0_ref[i] = res_t[i][LANES:] * mk(_MK_HEAD)

    Gs = [g_ref[b] for b in range(nb)]
    for c in range(nch):
        rows = slice(c * C, (c + 1) * C)
        for b in range(nb):
            y0_ref[b, rows, :] = _dot(b16(rp_ref[b, rows, :]), b16(Gs[b]), _NT) + y0_ref[b, rows, :]
        Gs = [_mm3(Gs[b], tm_ref[b * nch + c]) + g0_ref[b * nch + c] for b in range(nb)]
    for b in range(nb):
        g_ref[b] = Gs[b]

    gn_g, gn_b = gg_ref[...], gb_ref[...]
    for b in range(nb):
        y = y0_ref[b]
        yc = y - segsum(y) * (1.0 / HEAD_DIM)
        var = segsum(yc * yc) * (1.0 / HEAD_DIM)
        y_ref[b] = y_ref[b] + (yc * lax.rsqrt(var + GN_EPS) * gn_g + gn_b)


def _wkv(rkv, lw, a, k_k, k_a, r_k, gn_g, gn_b, *, batch, seq, tc=512):
    n = RWKV_DIM
    shp = (batch, seq, n)
    tc = min(tc, seq)
    nch = tc // CHUNK
    masks, tri = _wkv_masks()
    seq_spec = pl.BlockSpec((batch, tc, LANES), lambda p, i: (0, i, p))
    rkv_spec = lambda which: pl.BlockSpec((None, batch, tc, LANES), lambda p, i: (which, 0, i, p))
    par_spec = pl.BlockSpec((1, LANES), lambda p, i: (0, p))
    rkv = rkv.reshape((3,) + shp)
    out = pl.pallas_call(
        _wkv_body,
        grid=(n // LANES, seq // tc),
        in_specs=[rkv_spec(0), seq_spec, rkv_spec(1), rkv_spec(2), seq_spec] + [par_spec] * 5 + [
            pl.BlockSpec(masks.shape, lambda p, i: (0, 0, 0)),
            pl.BlockSpec(tri.shape, lambda p, i: (0, 0)),
        ],
        out_specs=seq_spec,
        out_shape=jax.ShapeDtypeStruct(shp, F32),
        scratch_shapes=[
            pltpu.VMEM((batch, LANES, LANES), F32),
            pltpu.VMEM((batch, tc, LANES), F32),
            pltpu.VMEM((batch, tc, LANES), F32),
            pltpu.VMEM((batch * nch, LANES, LANES), F32),
            pltpu.VMEM((batch * nch, LANES, LANES), F32),
        ],
        compiler_params=_cparams(("parallel", "arbitrary"), 32),
        name="wkv",
    )(rkv, lw.reshape(shp), rkv, rkv, a.reshape(shp), k_k, k_a, r_k, gn_g, gn_b, masks, tri)
    return out.reshape(batch * seq, n)


ATTN_TILE = ATTN_BLOCK * max(DILATIONS)
ATTN_GROUP = 4


def _dil_attn_body(q_ref, kc_ref, vc_ref, kp_ref, vp_ref, o_ref, m_ref, l_ref):
    blk = ATTN_BLOCK
    tile = q_ref.shape[0]
    first_lo = jnp.where(pl.program_id(2) == 0, blk, 0)

    lane = lax.broadcasted_iota(jnp.int32, (1, LANES), 1)
    h0 = lane < HEAD_DIM
    qi = lax.broadcasted_iota(jnp.int32, (blk, 2 * blk), 0)
    kj = lax.broadcasted_iota(jnp.int32, (blk, 2 * blk), 1)
    band = (kj >= qi) & (kj <= qi + blk)
    band_first = band & (kj >= first_lo)

    def rows(d, rho, n):
        start = rho + d * blk * n
        return slice(start, start + blk) if d == 1 else pl.ds(start, blk, stride=d)

    def scores(d, nblk, rho, n):
        q = q_ref[rows(d, rho, n), :].astype(BF16)
        if n == 0:
            kp, valid = kp_ref[rows(d, rho, nblk - 1), :], band_first
        else:
            kp, valid = kc_ref[rows(d, rho, n - 1), :], band
        keys = jnp.concatenate([kp, kc_ref[rows(d, rho, n), :]], 0).astype(BF16)
        zero = jnp.zeros_like(q)
        return [jnp.where(valid, lax.dot_general(jnp.where(hm, q, zero), keys, _NT, preferred_element_type=F32),
                          NEG_BIG) for hm in (h0, jnp.logical_not(h0))]

    def softmax(ss):
        ms = [jnp.max(s, axis=-1, keepdims=True) for s in ss]
        ps = [jnp.exp(s - m) for s, m in zip(ss, ms)]
        ls = [jnp.sum(p, axis=-1, keepdims=True) for p in ps]
        return (jnp.concatenate([p.astype(BF16) for p in ps], 1),
                jnp.where(h0, ms[0], ms[1]), jnp.where(h0, ls[0], ls[1]))

    def weighted(d, nblk, rho, n, p):
        vp = vp_ref[rows(d, rho, nblk - 1), :] if n == 0 else vc_ref[rows(d, rho, n - 1), :]
        vals = jnp.concatenate([vp, vc_ref[rows(d, rho, n), :]], 0).astype(BF16)
        zv = jnp.zeros_like(vals)
        vbd = jnp.concatenate([jnp.where(h0, vals, zv), jnp.where(h0, zv, vals)], 0)
        return jnp.dot(p, vbd, preferred_element_type=F32)

    for g, d in enumerate(DILATIONS):
        nblk = tile // (blk * d)
        items = [(rho, n) for rho in range(d) for n in range(nblk)]
        for i0 in range(0, len(items), ATTN_GROUP):
            grp = items[i0:i0 + ATTN_GROUP]
            ss = [scores(d, nblk, rho, n) for rho, n in grp]
            sm = [softmax(s) for s in ss]
            os_ = [weighted(d, nblk, rho, n, p) for (rho, n), (p, _, _) in zip(grp, sm)]
            for (rho, n), (_, mb, lb), o in zip(grp, sm, os_):
                out_rows = rows(d, rho, n)
                if g == 0:
                    o_ref[out_rows, :] = o
                    m_ref[out_rows, :] = mb
                    l_ref[out_rows, :] = lb
                else:
                    m_old = m_ref[out_rows, :]
                    m_new = jnp.maximum(m_old, mb)
                    a_old = jnp.exp(m_old - m_new)
                    a_blk = jnp.exp(mb - m_new)
                    o_ref[out_rows, :] = o_ref[out_rows, :] * a_old + o * a_blk
                    l_ref[out_rows, :] = l_ref[out_rows, :] * a_old + lb * a_blk
                    m_ref[out_rows, :] = m_new
    o_ref[...] = o_ref[...] / l_ref[...]


def _dil_attn(qkv, *, batch, seq):
    assert seq % ATTN_TILE == 0
    x = qkv.reshape(3, batch, seq, ATTN_DIM)

    def cur(which):
        return pl.BlockSpec((None, None, ATTN_TILE, LANES), lambda b, p, i: (which, b, i, p))

    def prev(which):
        return pl.BlockSpec((None, None, ATTN_TILE, LANES), lambda b, p, i: (which, b, jnp.maximum(i - 1, 0), p))

    o = pl.pallas_call(
        _dil_attn_body,
        grid=(batch, ATTN_DIM // LANES, seq // ATTN_TILE),
        in_specs=[cur(0), cur(1), cur(2), prev(1), prev(2)],
        out_specs=pl.BlockSpec((None, ATTN_TILE, LANES), lambda b, p, i: (b, i, p)),
        out_shape=jax.ShapeDtypeStruct((batch, seq, ATTN_DIM), F32),
        scratch_shapes=[pltpu.VMEM((ATTN_TILE, LANES), F32)] * 2,
        compiler_params=_cparams(("parallel", "parallel", "arbitrary"), 40),
        name="dil_attn",
    )(x, x, x, x, x)
    return o.reshape(batch * seq, ATTN_DIM)


def _out_ln_body(yw_ref, gl_ref, ya_ref, h_ref, g2_ref, wr_ref, wa_ref, g_ref, b_ref, out_ref):
    gate = jnp.dot(jax.nn.sigmoid(gl_ref[...]).astype(BF16), g2_ref[...], preferred_element_type=F32)
    yr = yw_ref[...] * gate
    mix = (jnp.dot(yr.astype(BF16), wr_ref[...], preferred_element_type=F32)
           + jnp.dot(ya_ref[...].astype(BF16), wa_ref[...], preferred_element_type=F32))
    z = ALPHA * h_ref[...] + mix
    out_ref[...] = _layer_norm(z, g_ref[...], b_ref[...])


def _out_ln(yw, glo, ya, h, g2, wo_r, wo_a, g, b, *, tm=512):
    m, d = h.shape
    rows = lambda w: pl.BlockSpec((tm, w), lambda i: (i, 0))
    full = lambda a: pl.BlockSpec(a.shape, lambda i: (0, 0))
    return pl.pallas_call(
        _out_ln_body,
        grid=(m // tm,),
        in_specs=[rows(RWKV_DIM), rows(GATE_LORA), rows(ATTN_DIM), rows(d),
                  full(g2), full(wo_r), full(wo_a), full(g), full(b)],
        out_specs=rows(d),
        out_shape=jax.ShapeDtypeStruct((m, d), F32),
        compiler_params=_cparams(("parallel",), 48),
        name="out_ln",
    )(yw, glo, ya, h, g2, wo_r, wo_a, g, b)


def _pad_rows(w, rows):
    return jnp.pad(w, ((0, rows - w.shape[0]), (0, 0)))


def _pad_cols(w, cols):
    return jnp.pad(w, ((0, 0), (0, cols - w.shape[1])))


def kernel(x, positions, ffn1_w_gate, ffn1_w_up, ffn1_w_down, ln1_g, ln1_b, w_in, mu_r, mu_k, mu_v, mu_w, mu_a, mu_g, w0, w2, a0, a2, g2, k_k, k_a, r_k, gn_g, gn_b, w_out, ln2_g, ln2_b, ffn2_w_gate, ffn2_w_up, ffn2_w_down, ln3_g, ln3_b):
    batch, seq, d = x.shape
    m = batch * seq
    h = x.reshape(m, d)
    cos_t, sin_t = _rope_tables(positions.astype(F32).reshape(m, 1))
    c = RWKV_DIM
    for l in range(DEPTH):
        row = lambda p: p[l].reshape(1, -1)
        h = _ffn_ln(h, ffn1_w_gate[l].astype(BF16), ffn1_w_up[l].astype(BF16),
                    ffn1_w_down[l].astype(BF16), row(ln1_g), row(ln1_b))

        wi = w_in[l]
        o_lo = 3 * c
        o_q = o_lo + DECAY_LORA + ICLR_LORA + GATE_LORA
        w_rkv = wi[:, :o_lo].astype(BF16)
        w_lora = jnp.concatenate([
            _pad_cols(wi[:, o_lo:o_lo + DECAY_LORA], LANES),
            _pad_cols(wi[:, o_lo + DECAY_LORA:o_lo + DECAY_LORA + ICLR_LORA], LANES),
            wi[:, o_lo + DECAY_LORA + ICLR_LORA:o_q]], axis=1).astype(BF16)
        w_qkv = wi[:, o_q:].reshape(d, 3, ATTN_DIM).transpose(1, 0, 2).astype(BF16)
        mu_l = jnp.concatenate([_pad_cols(row(mu_w), LANES), _pad_cols(row(mu_a), LANES), row(mu_g)], axis=1)
        mu_rkv = jnp.stack([row(mu_r), row(mu_k), row(mu_v)])

        qkv = _qkv_rope(h, w_qkv, cos_t, sin_t)
        rkv_s, lw, a_s, glo = _rwkv_proj(h, w_rkv, w_lora, mu_rkv, mu_l, row(w0), _pad_rows(w2[l], LANES),
                                         row(a0), _pad_rows(a2[l], LANES), seq=seq)
        yw = _wkv(rkv_s, lw, a_s, row(k_k), row(k_a), row(r_k), row(gn_g), row(gn_b), batch=batch, seq=seq)
        ya = _dil_attn(qkv, batch=batch, seq=seq)

        wo = w_out[l].astype(BF16)
        h = _out_ln(yw, glo, ya, h, g2[l].astype(BF16), wo[:c], wo[c:], row(ln2_g), row(ln2_b))

        h = _ffn_ln(h, ffn2_w_gate[l].astype(BF16), ffn2_w_up[l].astype(BF16),
                    ffn2_w_down[l].astype(BF16), row(ln3_g), row(ln3_b))
    return h.reshape(batch, seq, d)
```

```python
import functools

import numpy as np

import jax
import jax.numpy as jnp
from jax import lax
from jax.experimental import pallas as pl
from jax.experimental.pallas import tpu as pltpu

F32 = jnp.float32
BF16 = jnp.bfloat16

D_MODEL = 2048
HEAD_DIM = 64
ATTN_HEADS = 12
RWKV_HEADS = 20
ATTN_DIM = ATTN_HEADS * HEAD_DIM
RWKV_DIM = RWKV_HEADS * HEAD_DIM
DECAY_LORA = 96
ICLR_LORA = 96
GATE_LORA = 256
DILATIONS = (1, 4, 16)
ATTN_BLOCK = 128
ROPE_THETA = 500000.0
ROT_DIM = HEAD_DIM // 4
D_FF = 5632
DEPTH = 1
ALPHA = (2 * DEPTH) ** 0.25
LN_EPS = 1e-5
GN_EPS = 64e-5

LANES = 128
LORA_PAD = 512
CHUNK = 64
WKV_GROUPS = 1
NEG_BIG = -1e30
HI = lax.Precision.HIGHEST


def _cparams(sem, vmem_mb):
    return pltpu.CompilerParams(dimension_semantics=sem, vmem_limit_bytes=vmem_mb * 1024 * 1024)


def _layer_norm(z, g, b):
    mu = jnp.mean(z, axis=-1, keepdims=True)
    zc = z - mu
    var = jnp.mean(zc * zc, axis=-1, keepdims=True)
    return zc * lax.rsqrt(var + LN_EPS) * g + b


def _ffn_ln_body(x_ref, wg_ref, wu_ref, wd_ref, g_ref, b_ref, o_ref, xb_ref, acc_ref, *, nf):
    j = pl.program_id(1)

    @pl.when(j == 0)
    def _():
        xb_ref[...] = x_ref[...].astype(BF16)
        acc_ref[...] = jnp.zeros_like(acc_ref)

    xb = xb_ref[...]
    gate = jnp.dot(xb, wg_ref[...], preferred_element_type=F32)
    up = jnp.dot(xb, wu_ref[...], preferred_element_type=F32)
    hmid = (gate * jax.nn.sigmoid(gate)) * up
    acc_ref[...] += jnp.dot(hmid.astype(BF16), wd_ref[...], preferred_element_type=F32)

    @pl.when(j == nf - 1)
    def _():
        z = ALPHA * x_ref[...] + 0.5 * acc_ref[...]
        o_ref[...] = _layer_norm(z, g_ref[...], b_ref[...])


def _ffn_ln(x, wg, wu, wd, g, b, *, tm=512, tf=512):
    m, d = x.shape
    dff = wg.shape[1]
    nf = dff // tf
    return pl.pallas_call(
        functools.partial(_ffn_ln_body, nf=nf),
        grid=(m // tm, nf),
        in_specs=[
            pl.BlockSpec((tm, d), lambda i, j: (i, 0)),
            pl.BlockSpec((d, tf), lambda i, j: (0, j)),
            pl.BlockSpec((d, tf), lambda i, j: (0, j)),
            pl.BlockSpec((tf, d), lambda i, j: (j, 0)),
            pl.BlockSpec((1, d), lambda i, j: (0, 0)),
            pl.BlockSpec((1, d), lambda i, j: (0, 0)),
        ],
        out_specs=pl.BlockSpec((tm, d), lambda i, j: (i, 0)),
        out_shape=jax.ShapeDtypeStruct((m, d), F32),
        scratch_shapes=[pltpu.VMEM((tm, d), BF16), pltpu.VMEM((tm, d), F32)],
        compiler_params=_cparams(("parallel", "arbitrary"), 48),
        name="ffn_ln",
    )(x, wg, wu, wd, g, b)


def _rope_tables_body(pos_ref, invf_ref, rotm_ref, cos_ref, sin_ref):
    ang = pos_ref[...] * invf_ref[...]
    rot = rotm_ref[...] > 0.0
    cos_ref[...] = jnp.where(rot, jnp.cos(ang), 1.0)
    sin_ref[...] = jnp.where(rot, jnp.sin(ang), 0.0)


def _rope_tables(pos_f32, *, tm=1024):
    m = pos_f32.shape[0]
    half = ROT_DIM // 2
    inv_freq = jnp.power(ROPE_THETA, -jnp.arange(half, dtype=F32) * (2.0 / ROT_DIM))
    l64 = jnp.arange(LANES) % HEAD_DIM
    invf = jnp.where(l64 < ROT_DIM, inv_freq[l64 % half], 0.0).astype(F32)[None, :]
    rotm = (l64 < ROT_DIM).astype(F32)[None, :]
    return pl.pallas_call(
        _rope_tables_body,
        grid=(m // tm,),
        in_specs=[
            pl.BlockSpec((tm, 1), lambda i: (i, 0)),
            pl.BlockSpec((1, LANES), lambda i: (0, 0)),
            pl.BlockSpec((1, LANES), lambda i: (0, 0)),
        ],
        out_specs=[pl.BlockSpec((tm, LANES), lambda i: (i, 0))] * 2,
        out_shape=[jax.ShapeDtypeStruct((m, LANES), F32)] * 2,
        compiler_params=_cparams(("parallel",), 16),
        name="rope_tables",
    )(pos_f32, invf, rotm)


def _qkv_rope_body(x_ref, w_ref, cos_ref, sin_ref, o_ref):
    xb = x_ref[...].astype(BF16)
    c = cos_ref[...]
    s = sin_ref[...]
    l64 = lax.broadcasted_iota(jnp.int32, (1, LANES), 1) & (HEAD_DIM - 1)
    lo = l64 < ROT_DIM // 2
    hi = l64 < ROT_DIM
    for which in range(3):
        y = jnp.dot(xb, w_ref[:, which * ATTN_DIM:(which + 1) * ATTN_DIM], preferred_element_type=F32)
        if which == 2:
            o_ref[which] = y
            continue
        for t in range(ATTN_DIM // LANES):
            yt = y[:, t * LANES:(t + 1) * LANES]
            up = pltpu.roll(yt, LANES - ROT_DIM // 2, axis=1)
            dn = pltpu.roll(yt, ROT_DIM // 2, axis=1)
            rot = jnp.where(lo, -up, jnp.where(hi, dn, 0.0))
            yt = yt * c + rot * s
            if which == 0:
                yt = yt * HEAD_DIM ** -0.5
            o_ref[which, :, t * LANES:(t + 1) * LANES] = yt


def _qkv_rope(x, w, cos_t, sin_t, *, tm=512):
    m, d = x.shape
    n = w.shape[1] // 3
    return pl.pallas_call(
        _qkv_rope_body,
        grid=(m // tm,),
        in_specs=[
            pl.BlockSpec((tm, d), lambda i: (i, 0)),
            pl.BlockSpec(w.shape, lambda i: (0, 0), pipeline_mode=pl.Buffered(1)),
            pl.BlockSpec((tm, LANES), lambda i: (i, 0)),
            pl.BlockSpec((tm, LANES), lambda i: (i, 0)),
        ],
        out_specs=pl.BlockSpec((3, tm, n), lambda i: (0, i, 0)),
        out_shape=jax.ShapeDtypeStruct((3, m, n), F32),
        compiler_params=_cparams(("parallel",), 40),
        name="qkv_rope",
    )(x, w, cos_t, sin_t)


def _rwkv_proj_body(x_ref, w_ref, mu_ref, w0_ref, w2_ref, a0_ref, a2_ref,
                    rkv_ref, lw_ref, ao_ref, go_ref, carry_ref, *, tiles_per_seq):
    tm = x_ref.shape[0]
    n = RWKV_DIM
    row0 = lax.broadcasted_iota(jnp.int32, (tm, 1), 0) == 0

    @pl.when((pl.program_id(0) % tiles_per_seq) == 0)
    def _():
        carry_ref[...] = jnp.zeros_like(carry_ref)

    xb = x_ref[...].astype(BF16)

    def shifted(lo, hi):
        z = jnp.dot(xb, w_ref[:, lo:hi], preferred_element_type=F32)
        prev = jnp.where(row0, carry_ref[0:1, lo:hi], pltpu.roll(z, 1, axis=0))
        carry_ref[0:1, lo:hi] = z[tm - 1:tm, :]
        return z + (prev - z) * mu_ref[:, lo:hi]

    for which in range(3):
        rkv_ref[which] = shifted(which * n, (which + 1) * n)
    ls = shifted(3 * n, 3 * n + LORA_PAD)
    w_lo = ls[:, 0:LANES]
    a_lo = ls[:, LANES:2 * LANES]
    go_ref[...] = ls[:, 2 * LANES:]
    w = w0_ref[...] + jnp.dot(jnp.tanh(w_lo).astype(BF16), w2_ref[...], preferred_element_type=F32)
    x = -w
    softplus = jnp.maximum(x, 0.0) + jnp.log(1.0 + jnp.exp(-jnp.abs(x)))
    lw_ref[...] = -jnp.exp(-softplus - 0.5)
    a = a0_ref[...] + jnp.dot(a_lo.astype(BF16), a2_ref[...], preferred_element_type=F32)
    ao_ref[...] = jax.nn.sigmoid(a)


def _rwkv_proj(x, w_all, mu_all, w0, w2p, a0, a2p, *, seq, tm=256):
    m, d = x.shape
    n = RWKV_DIM
    const = lambda a: pl.BlockSpec(a.shape, lambda i: (0, 0), pipeline_mode=pl.Buffered(1))
    return pl.pallas_call(
        functools.partial(_rwkv_proj_body, tiles_per_seq=seq // tm),
        grid=(m // tm,),
        in_specs=[pl.BlockSpec((tm, d), lambda i: (i, 0)),
                  const(w_all), const(mu_all), const(w0), const(w2p), const(a0), const(a2p)],
        out_specs=[pl.BlockSpec((3, tm, n), lambda i: (0, i, 0))]
        + [pl.BlockSpec((tm, n), lambda i: (i, 0))] * 2
        + [pl.BlockSpec((tm, GATE_LORA), lambda i: (i, 0))],
        out_shape=[jax.ShapeDtypeStruct((3, m, n), F32)]
        + [jax.ShapeDtypeStruct((m, n), F32)] * 2
        + [jax.ShapeDtypeStruct((m, GATE_LORA), F32)],
        scratch_shapes=[pltpu.VMEM((8, w_all.shape[1]), F32)],
        compiler_params=_cparams(("arbitrary",), 48),
        name="rwkv_proj",
    )(x, w_all, mu_all, w0, w2p, a0, a2p)


_NN = (((1,), (0,)), ((), ()))
_NT = (((1,), (1,)), ((), ()))
_TN = (((0,), (0,)), ((), ()))


def _split(a):
    hi = a.astype(BF16)
    return hi, (a - hi.astype(F32)).astype(BF16)


def _dot(a, b, dims=_NN):
    return lax.dot_general(a, b, dims, preferred_element_type=F32)


def _mm3(a, b, dims=_NN):
    ah, al = _split(a)
    bh, bl = _split(b)
    return _dot(ah, bh, dims) + (_dot(ah, bl, dims) + _dot(al, bh, dims))


def _cumsum_rows(tri, x):
    hi, lo = _split(x)
    lo2 = (x - hi.astype(F32) - lo.astype(F32)).astype(BF16)
    return _dot(tri, hi) + (_dot(tri, lo) + _dot(tri, lo2))


_MK_M0, _MK_M1, _MK_TL, _MK_BR, _MK_EYE, _MK_HEAD, _MK_LVL = 0, 1, 2, 3, 4, 5, 6


def _wkv_masks():
    C = CHUNK
    ri, ci = np.meshgrid(np.arange(2 * C), np.arange(2 * C), indexing="ij")
    ti, sj = ri % C, ci % C
    top, left = ri < C, ci < C
    masks = [
        (sj < ti) | (~top & (sj == ti)),
        (sj < ti) | (top & (sj == ti)),
        top & left,
        ~top & ~left,
        ri == ci,
        (ri < HEAD_DIM) == (ci < HEAD_DIM),
    ]
    s = 1
    while s < C:
        masks.append((ri // (2 * s) == ci // (2 * s)) & ((ri // s) % 2 == 1) & ((ci // s) % 2 == 0))
        s *= 2
    tri = np.tril(np.ones((C, C), np.float32))
    return jnp.asarray(np.stack(masks).astype(np.float32)), jnp.asarray(tri, dtype=BF16)


def _wkv_body(r_ref, lw_ref, k_ref, v_ref, a_ref, kk_ref, ka_ref, rk_ref, gg_ref, gb_ref,
              mk_ref, tri_ref, y_ref, g_ref, rp_ref, y0_ref, tm_ref, g0_ref):
    C = CHUNK
    tc = r_ref.shape[1]
    nch = tc // C
    seqs = [(b, slice(q * LANES, (q + 1) * LANES))
            for b in range(r_ref.shape[0]) for q in range(r_ref.shape[2] // LANES)]
    nb = len(seqs)

    @pl.when(pl.program_id(1) == 0)
    def _():
        g_ref[...] = jnp.zeros_like(g_ref)

    lane = lax.broadcasted_iota(jnp.int32, (1, LANES), 1)
    h0 = lane < HEAD_DIM
    m0 = jnp.where(h0, 1.0, 0.0)
    m1 = 1.0 - m0
    zeros_c = jnp.zeros((C, LANES), BF16)
    tri = tri_ref[...]
    mk = lambda i: mk_ref[i]

    def segsum(x):
        s0 = jnp.sum(x * m0, axis=-1, keepdims=True)
        s1 = jnp.sum(x * m1, axis=-1, keepdims=True)
        return jnp.where(h0, s0, s1)

    b16 = lambda x: x.astype(BF16)
    cat0 = lambda *xs: jnp.concatenate(xs, 0)
    cat1 = lambda *xs: jnp.concatenate(xs, 1)

    each = lambda f, *ls: [f(*xs) for xs in zip(*ls)]

    def y_rhs(ap, xw, x0, x1):
        return cat0(cat1(b16(ap * m0), b16(xw * m0)), cat1(zeros_c, x0), cat1(zeros_c, x1),
                    cat1(b16(ap * m1), b16(xw * m1)))

    def build(chains):
        ch_r, ch_lw, ch_k, ch_v, ch_a, ch_b = [], [], [], [], [], []
        for s, c in chains:
            b, ln = seqs[s]
            rows = slice(c * C, (c + 1) * C)
            r, ks, v, a = r_ref[b, rows, ln], k_ref[b, rows, ln], v_ref[b, rows, ln], a_ref[b, rows, ln]
            kk = ks * kk_ref[:, ln]
            kk = kk / jnp.maximum(jnp.sqrt(segsum(kk * kk)), 1e-12)
            kmod = ks * (1.0 + (a - 1.0) * ka_ref[:, ln])
            y_ref[b, rows, ln] = segsum(r * kmod * rk_ref[:, ln]) * v
            ch_r.append(r); ch_lw.append(lw_ref[b, rows, ln]); ch_k.append(kmod)
            ch_v.append(v); ch_a.append(-kk); ch_b.append(kk * a)

        cs = each(lambda lw: _cumsum_rows(tri, lw), ch_lw)
        cl = each(lambda s: s[C - 1:C, :], cs)
        At = each(lambda av, s, lw: av * jnp.exp(s - lw), ch_a, cs, ch_lw)
        Rt = each(lambda r, s: r * jnp.exp(s), ch_r, cs)
        pinv = each(lambda s: jnp.exp(-s), cs)
        Bt = each(lambda bv, p: b16(bv * p), ch_b, pinv)
        Kt = each(lambda k, p: b16(k * p), ch_k, pinv)
        pend = each(lambda l, s: jnp.exp(l - s), cl, cs)
        BKh = each(lambda bv, k, p: b16(cat0(bv * p, k * p)), ch_b, ch_k, pend)
        At0 = each(lambda x: b16(x * m0), At)
        At1 = each(lambda x: b16(x * m1), At)

        M0 = each(lambda a0, rt, bt, kt: _dot(cat0(a0, b16(rt * m0)), cat0(bt, kt), _NT), At0, Rt, Bt, Kt)
        M1 = each(lambda a1, rt, bt, kt: _dot(cat0(b16(rt * m1), a1), cat0(kt, bt), _NT), At1, Rt, Bt, Kt)
        M0 = each(lambda x: jnp.where(mk(_MK_M0) > 0.5, x, 0.0), M0)
        M1 = each(lambda x: jnp.where(mk(_MK_M1) > 0.5, x, 0.0), M1)
        M0b, M1b = each(b16, M0), each(b16, M1)

        Lbd = each(lambda x0, x1: x0 * mk(_MK_TL) + x1 * mk(_MK_BR), M0, M1)
        X = each(lambda l: mk(_MK_EYE) + l * mk(_MK_LVL), Lbd)
        for lv in range(1, C.bit_length() - 1):
            Xb = each(b16, X)
            XC = each(lambda xb, l: b16(_dot(xb, b16(l * mk(_MK_LVL + lv)))), Xb, Lbd)
            X = each(lambda x, xc, xb: x + _dot(xc, xb), X, XC, Xb)
        Xb = each(b16, X)

        v0 = each(lambda v: b16(v * m0), ch_v)
        v1 = each(lambda v: b16(v * m1), ch_v)
        W = each(lambda a, b_, x0, x1: _dot(cat1(a[:C], b_[C:]), cat0(zeros_c, x0, x1, zeros_c)),
                 M0b, M1b, v0, v1)
        res = each(lambda xb, a0, a1, w: _dot(xb, cat0(cat1(a0, b16(w * m0)), cat1(a1, b16(w * m1)))),
                   Xb, At0, At1, W)
        Ap = each(lambda x: x[:C, :LANES] + x[C:, :LANES], res)
        XW = each(lambda x: x[:C, LANES:] + x[C:, LANES:], res)
        res_y = each(lambda a, b_, ap, xw, x0, x1: _dot(cat1(a[C:], b_[:C]), y_rhs(ap, xw, x0, x1)),
                     M0b, M1b, Ap, XW, v0, v1)
        res_t = each(lambda ap, xw, v, bk: _dot(cat0(cat1(b16(ap), b16(xw)), cat1(zeros_c, b16(v))), bk, _TN),
                     Ap, XW, ch_v, BKh)
        for i, (s, c) in enumerate(chains):
            rows = slice(c * C, (c + 1) * C)
            rp_ref[s, rows, :] = Rt[i] + res_y[i][:, :LANES]
            y0_ref[s, rows, :] = res_y[i][:, LANES:]
            tm_ref[s * nch + c] = (res_t[i][:LANES] + mk(_MK_EYE) * jnp.exp(cl[i])) * mk(_MK_HEAD)
            g0_ref[s * nch + c] = res_t[i][LANES:] * mk(_MK_HEAD)

    chains = [(s, c) for c in range(nch) for s in range(nb)]
    per_group = -(-len(chains) // WKV_GROUPS)
    for g0 in range(0, len(chains), per_group):
        build(chains[g0:g0 + per_group])

    Gs = [g_ref[b] for b in range(nb)]
    for c in range(nch):
        rows = slice(c * C, (c + 1) * C)
        for b in range(nb):
            y0_ref[b, rows, :] = _dot(b16(rp_ref[b, rows, :]), b16(Gs[b]), _NT) + y0_ref[b, rows, :]
        Gs = [_mm3(Gs[b], tm_ref[b * nch + c]) + g0_ref[b * nch + c] for b in range(nb)]
    for b in range(nb):
        g_ref[b] = Gs[b]

    for s, (b, ln) in enumerate(seqs):
        y = y0_ref[s]
        yc = y - segsum(y) * (1.0 / HEAD_DIM)
        var = segsum(yc * yc) * (1.0 / HEAD_DIM)
        y_ref[b, :, ln] = y_ref[b, :, ln] + (yc * lax.rsqrt(var + GN_EPS) * gg_ref[:, ln] + gb_ref[:, ln])


def _wkv(rkv, lw, a, k_k, k_a, r_k, gn_g, gn_b, *, batch, seq, tc=128, tiles=5):
    n = RWKV_DIM
    shp = (batch, seq, n)
    tc = min(tc, seq)
    nch = tc // CHUNK
    nseq = batch * tiles
    wl = tiles * LANES
    masks, tri = _wkv_masks()
    seq_spec = pl.BlockSpec((batch, tc, wl), lambda p, i: (0, i, p))
    rkv_spec = lambda which: pl.BlockSpec((None, batch, tc, wl), lambda p, i: (which, 0, i, p))
    par_spec = pl.BlockSpec((1, wl), lambda p, i: (0, p))
    rkv = rkv.reshape((3,) + shp)
    out = pl.pallas_call(
        _wkv_body,
        grid=(n // wl, seq // tc),
        in_specs=[rkv_spec(0), seq_spec, rkv_spec(1), rkv_spec(2), seq_spec] + [par_spec] * 5 + [
            pl.BlockSpec(masks.shape, lambda p, i: (0, 0, 0)),
            pl.BlockSpec(tri.shape, lambda p, i: (0, 0)),
        ],
        out_specs=seq_spec,
        out_shape=jax.ShapeDtypeStruct(shp, F32),
        scratch_shapes=[
            pltpu.VMEM((nseq, LANES, LANES), F32),
            pltpu.VMEM((nseq, tc, LANES), F32),
            pltpu.VMEM((nseq, tc, LANES), F32),
            pltpu.VMEM((nseq * nch, LANES, LANES), F32),
            pltpu.VMEM((nseq * nch, LANES, LANES), F32),
        ],
        compiler_params=_cparams(("parallel", "arbitrary"), 32),
        name="wkv",
    )(rkv, lw.reshape(shp), rkv, rkv, a.reshape(shp), k_k, k_a, r_k, gn_g, gn_b, masks, tri)
    return out.reshape(batch * seq, n)


ATTN_TILE = ATTN_BLOCK * max(DILATIONS)
ATTN_GROUP = 4


def _dil_attn_body(q_ref, kc_ref, vc_ref, kp_ref, vp_ref, o_ref, m_ref, l_ref):
    blk = ATTN_BLOCK
    tile = q_ref.shape[0]
    first_lo = jnp.where(pl.program_id(2) == 0, blk, 0)

    lane = lax.broadcasted_iota(jnp.int32, (1, LANES), 1)
    h0 = lane < HEAD_DIM
    qi = lax.broadcasted_iota(jnp.int32, (blk, 2 * blk), 0)
    kj = lax.broadcasted_iota(jnp.int32, (blk, 2 * blk), 1)
    band = (kj >= qi) & (kj <= qi + blk)
    band_first = band & (kj >= first_lo)

    def rows(d, rho, n):
        start = rho + d * blk * n
        return slice(start, start + blk) if d == 1 else pl.ds(start, blk, stride=d)

    def scores(d, nblk, rho, n):
        q = q_ref[rows(d, rho, n), :].astype(BF16)
        if n == 0:
            kp, valid = kp_ref[rows(d, rho, nblk - 1), :], band_first
        else:
            kp, valid = kc_ref[rows(d, rho, n - 1), :], band
        keys = jnp.concatenate([kp, kc_ref[rows(d, rho, n), :]], 0).astype(BF16)
        zero = jnp.zeros_like(q)
        return [jnp.where(valid, lax.dot_general(jnp.where(hm, q, zero), keys, _NT, preferred_element_type=F32),
                          NEG_BIG) for hm in (h0, jnp.logical_not(h0))]

    def softmax(ss):
        ms = [jnp.max(s, axis=-1, keepdims=True) for s in ss]
        ps = [jnp.exp(s - m) for s, m in zip(ss, ms)]
        ls = [jnp.sum(p, axis=-1, keepdims=True) for p in ps]
        return (jnp.concatenate([p.astype(BF16) for p in ps], 1),
                jnp.where(h0, ms[0], ms[1]), jnp.where(h0, ls[0], ls[1]))

    def weighted(d, nblk, rho, n, p):
        vp = vp_ref[rows(d, rho, nblk - 1), :] if n == 0 else vc_ref[rows(d, rho, n - 1), :]
        vals = jnp.concatenate([vp, vc_ref[rows(d, rho, n), :]], 0).astype(BF16)
        zv = jnp.zeros_like(vals)
        vbd = jnp.concatenate([jnp.where(h0, vals, zv), jnp.where(h0, zv, vals)], 0)
        return jnp.dot(p, vbd, preferred_element_type=F32)

    for g, d in enumerate(DILATIONS):
        nblk = tile // (blk * d)
        items = [(rho, n) for rho in range(d) for n in range(nblk)]
        for i0 in range(0, len(items), ATTN_GROUP):
            grp = items[i0:i0 + ATTN_GROUP]
            ss = [scores(d, nblk, rho, n) for rho, n in grp]
            sm = [softmax(s) for s in ss]
            os_ = [weighted(d, nblk, rho, n, p) for (rho, n), (p, _, _) in zip(grp, sm)]
            for (rho, n), (_, mb, lb), o in zip(grp, sm, os_):
                out_rows = rows(d, rho, n)
                if g == 0:
                    o_ref[out_rows, :] = o
                    m_ref[out_rows, :] = mb
                    l_ref[out_rows, :] = lb
                else:
                    m_old = m_ref[out_rows, :]
                    m_new = jnp.maximum(m_old, mb)
                    a_old = jnp.exp(m_old - m_new)
                    a_blk = jnp.exp(mb - m_new)
                    o_ref[out_rows, :] = o_ref[out_rows, :] * a_old + o * a_blk
                    l_ref[out_rows, :] = l_ref[out_rows, :] * a_old + lb * a_blk
                    m_ref[out_rows, :] = m_new
    o_ref[...] = o_ref[...] / l_ref[...]


def _dil_attn(qkv, *, batch, seq):
    assert seq % ATTN_TILE == 0
    x = qkv.reshape(3, batch, seq, ATTN_DIM)

    def cur(which):
        return pl.BlockSpec((None, None, ATTN_TILE, LANES), lambda b, p, i: (which, b, i, p))

    def prev(which):
        return pl.BlockSpec((None, None, ATTN_TILE, LANES), lambda b, p, i: (which, b, jnp.maximum(i - 1, 0), p))

    o = pl.pallas_call(
        _dil_attn_body,
        grid=(batch, ATTN_DIM // LANES, seq // ATTN_TILE),
        in_specs=[cur(0), cur(1), cur(2), prev(1), prev(2)],
        out_specs=pl.BlockSpec((None, ATTN_TILE, LANES), lambda b, p, i: (b, i, p)),
        out_shape=jax.ShapeDtypeStruct((batch, seq, ATTN_DIM), F32),
        scratch_shapes=[pltpu.VMEM((ATTN_TILE, LANES), F32)] * 2,
        compiler_params=_cparams(("parallel", "parallel", "arbitrary"), 40),
        name="dil_attn",
    )(x, x, x, x, x)
    return o.reshape(batch * seq, ATTN_DIM)


def _out_ln_body(yw_ref, gl_ref, ya_ref, h_ref, g2_ref, wr_ref, wa_ref, g_ref, b_ref, out_ref):
    gate = jnp.dot(jax.nn.sigmoid(gl_ref[...]).astype(BF16), g2_ref[...], preferred_element_type=F32)
    yr = yw_ref[...] * gate
    mix = (jnp.dot(yr.astype(BF16), wr_ref[...], preferred_element_type=F32)
           + jnp.dot(ya_ref[...].astype(BF16), wa_ref[...], preferred_element_type=F32))
    z = ALPHA * h_ref[...] + mix
    out_ref[...] = _layer_norm(z, g_ref[...], b_ref[...])


def _out_ln(yw, glo, ya, h, g2, wo_r, wo_a, g, b, *, tm=512):
    m, d = h.shape
    rows = lambda w: pl.BlockSpec((tm, w), lambda i: (i, 0))
    full = lambda a: pl.BlockSpec(a.shape, lambda i: (0, 0))
    return pl.pallas_call(
        _out_ln_body,
        grid=(m // tm,),
        in_specs=[rows(RWKV_DIM), rows(GATE_LORA), rows(ATTN_DIM), rows(d),
                  full(g2), full(wo_r), full(wo_a), full(g), full(b)],
        out_specs=rows(d),
        out_shape=jax.ShapeDtypeStruct((m, d), F32),
        compiler_params=_cparams(("parallel",), 48),
        name="out_ln",
    )(yw, glo, ya, h, g2, wo_r, wo_a, g, b)


def _pad_rows(w, rows):
    return jnp.pad(w, ((0, rows - w.shape[0]), (0, 0)))


def _pad_cols(w, cols):
    return jnp.pad(w, ((0, 0), (0, cols - w.shape[1])))


def kernel(x, positions, ffn1_w_gate, ffn1_w_up, ffn1_w_down, ln1_g, ln1_b, w_in, mu_r, mu_k, mu_v, mu_w, mu_a, mu_g, w0, w2, a0, a2, g2, k_k, k_a, r_k, gn_g, gn_b, w_out, ln2_g, ln2_b, ffn2_w_gate, ffn2_w_up, ffn2_w_down, ln3_g, ln3_b):
    batch, seq, d = x.shape
    m = batch * seq
    h = x.reshape(m, d)
    cos_t, sin_t = _rope_tables(positions.astype(F32).reshape(m, 1))
    c = RWKV_DIM
    for l in range(DEPTH):
        row = lambda p: p[l].reshape(1, -1)
        h = _ffn_ln(h, ffn1_w_gate[l].astype(BF16), ffn1_w_up[l].astype(BF16),
                    ffn1_w_down[l].astype(BF16), row(ln1_g), row(ln1_b))

        wi = w_in[l]
        o_lo = 3 * c
        o_q = o_lo + DECAY_LORA + ICLR_LORA + GATE_LORA
        w_rwkv = jnp.concatenate([
            wi[:, :o_lo],
            _pad_cols(wi[:, o_lo:o_lo + DECAY_LORA], LANES),
            _pad_cols(wi[:, o_lo + DECAY_LORA:o_lo + DECAY_LORA + ICLR_LORA], LANES),
            wi[:, o_lo + DECAY_LORA + ICLR_LORA:o_q]], axis=1).astype(BF16)
        mu_rwkv = jnp.concatenate([row(mu_r), row(mu_k), row(mu_v), _pad_cols(row(mu_w), LANES),
                                   _pad_cols(row(mu_a), LANES), row(mu_g)], axis=1)
        w_qkv = wi[:, o_q:].astype(BF16)

        qkv = _qkv_rope(h, w_qkv, cos_t, sin_t)
        rkv_s, lw, a_s, glo = _rwkv_proj(h, w_rwkv, mu_rwkv, row(w0), _pad_rows(w2[l], LANES).astype(BF16),
                                         row(a0), _pad_rows(a2[l], LANES).astype(BF16), seq=seq)
        yw = _wkv(rkv_s, lw, a_s, row(k_k), row(k_a), row(r_k), row(gn_g), row(gn_b), batch=batch, seq=seq)
        ya = _dil_attn(qkv, batch=batch, seq=seq)

        wo = w_out[l].astype(BF16)
        h = _out_ln(yw, glo, ya, h, g2[l].astype(BF16), wo[:c], wo[c:], row(ln2_g), row(ln2_b))

        h = _ffn_ln(h, ffn2_w_gate[l].astype(BF16), ffn2_w_up[l].astype(BF16),
                    ffn2_w_down[l].astype(BF16), row(ln3_g), row(ln3_b))
    return h.reshape(batch, seq, d)
```

```python
import functools

import numpy as np

import jax
import jax.numpy as jnp
from jax import lax
from jax.experimental import pallas as pl
from jax.experimental.pallas import tpu as pltpu

F32 = jnp.float32
BF16 = jnp.bfloat16

D_MODEL = 2048
HEAD_DIM = 64
ATTN_HEADS = 12
RWKV_HEADS = 20
ATTN_DIM = ATTN_HEADS * HEAD_DIM
RWKV_DIM = RWKV_HEADS * HEAD_DIM
DECAY_LORA = 96
ICLR_LORA = 96
GATE_LORA = 256
DILATIONS = (1, 4, 16)
ATTN_BLOCK = 128
ROPE_THETA = 500000.0
ROT_DIM = HEAD_DIM // 4
D_FF = 5632
DEPTH = 1
ALPHA = (2 * DEPTH) ** 0.25
LN_EPS = 1e-5
GN_EPS = 64e-5

LANES = 128
BF16_SUBLANES = 16
LORA_PAD = 512
CHUNK = 64
WKV_GROUPS = 2
NEG_BIG = -1e30
HI = lax.Precision.HIGHEST


def _cparams(sem, vmem_mb):
    return pltpu.CompilerParams(dimension_semantics=sem, vmem_limit_bytes=vmem_mb * 1024 * 1024)


def _layer_norm(z, g, b):
    mu = jnp.mean(z, axis=-1, keepdims=True)
    zc = z - mu
    var = jnp.mean(zc * zc, axis=-1, keepdims=True)
    return zc * lax.rsqrt(var + LN_EPS) * g + b


def _ffn_ln_body(x_ref, wg_ref, wu_ref, wd_ref, g_ref, b_ref, o_ref, xb_ref, *, nf):
    j = pl.program_id(1)

    @pl.when(j == 0)
    def _():
        xb_ref[...] = x_ref[...].astype(BF16)
        o_ref[...] = jnp.zeros_like(o_ref)

    xb = xb_ref[...]
    gate = jnp.dot(xb, wg_ref[...], preferred_element_type=F32)
    up = jnp.dot(xb, wu_ref[...], preferred_element_type=F32)
    hmid = (gate * jax.nn.sigmoid(gate)) * up
    o_ref[...] += jnp.dot(hmid.astype(BF16), wd_ref[...], preferred_element_type=F32)

    @pl.when(j == nf - 1)
    def _():
        z = ALPHA * x_ref[...] + 0.5 * o_ref[...]
        o_ref[...] = _layer_norm(z, g_ref[...], b_ref[...])


def _ffn_ln(x, wg, wu, wd, g, b, *, tm=512, tf=512):
    m, d = x.shape
    dff = wg.shape[1]
    nf = dff // tf
    return pl.pallas_call(
        functools.partial(_ffn_ln_body, nf=nf),
        grid=(m // tm, nf),
        in_specs=[
            pl.BlockSpec((tm, d), lambda i, j: (i, 0)),
            pl.BlockSpec((d, tf), lambda i, j: (0, j)),
            pl.BlockSpec((d, tf), lambda i, j: (0, j)),
            pl.BlockSpec((tf, d), lambda i, j: (j, 0)),
            pl.BlockSpec((1, d), lambda i, j: (0, 0)),
            pl.BlockSpec((1, d), lambda i, j: (0, 0)),
        ],
        out_specs=pl.BlockSpec((tm, d), lambda i, j: (i, 0)),
        out_shape=jax.ShapeDtypeStruct((m, d), F32),
        scratch_shapes=[pltpu.VMEM((tm, d), BF16)],
        compiler_params=_cparams(("parallel", "arbitrary"), 40),
        name="ffn_ln",
    )(x, wg, wu, wd, g, b)


def _rope_tables_body(pos_ref, invf_ref, rotm_ref, cos_ref, sin_ref):
    ang = pos_ref[...] * invf_ref[...]
    rot = rotm_ref[...] > 0.0
    cos_ref[...] = jnp.where(rot, jnp.cos(ang), 1.0)
    sin_ref[...] = jnp.where(rot, jnp.sin(ang), 0.0)


def _rope_tables(pos_f32, *, tm=1024):
    m = pos_f32.shape[0]
    half = ROT_DIM // 2
    inv_freq = jnp.power(ROPE_THETA, -jnp.arange(half, dtype=F32) * (2.0 / ROT_DIM))
    l64 = jnp.arange(LANES) % HEAD_DIM
    invf = jnp.where(l64 < ROT_DIM, inv_freq[l64 % half], 0.0).astype(F32)[None, :]
    rotm = (l64 < ROT_DIM).astype(F32)[None, :]
    return pl.pallas_call(
        _rope_tables_body,
        grid=(m // tm,),
        in_specs=[
            pl.BlockSpec((tm, 1), lambda i: (i, 0)),
            pl.BlockSpec((1, LANES), lambda i: (0, 0)),
            pl.BlockSpec((1, LANES), lambda i: (0, 0)),
        ],
        out_specs=[pl.BlockSpec((tm, LANES), lambda i: (i, 0))] * 2,
        out_shape=[jax.ShapeDtypeStruct((m, LANES), F32)] * 2,
        compiler_params=_cparams(("parallel",), 16),
        name="rope_tables",
    )(pos_f32, invf, rotm)


def _qkv_rope_body(x_ref, w_ref, cos_ref, sin_ref, o_ref):
    xb = x_ref[...].astype(BF16)
    c = cos_ref[...]
    s = sin_ref[...]
    l64 = lax.broadcasted_iota(jnp.int32, (1, LANES), 1) & (HEAD_DIM - 1)
    lo = l64 < ROT_DIM // 2
    hi = l64 < ROT_DIM
    for which in range(3):
        y = jnp.dot(xb, w_ref[:, which * ATTN_DIM:(which + 1) * ATTN_DIM], preferred_element_type=F32)
        if which == 2:
            o_ref[which] = y
            continue
        for t in range(ATTN_DIM // LANES):
            yt = y[:, t * LANES:(t + 1) * LANES]
            up = pltpu.roll(yt, LANES - ROT_DIM // 2, axis=1)
            dn = pltpu.roll(yt, ROT_DIM // 2, axis=1)
            rot = jnp.where(lo, -up, jnp.where(hi, dn, 0.0))
            yt = yt * c + rot * s
            if which == 0:
                yt = yt * (HEAD_DIM ** -0.5 * LOG2E)
            o_ref[which, :, t * LANES:(t + 1) * LANES] = yt


def _qkv_rope(x, w, cos_t, sin_t, *, tm=512):
    m, d = x.shape
    n = w.shape[1] // 3
    return pl.pallas_call(
        _qkv_rope_body,
        grid=(m // tm,),
        in_specs=[
            pl.BlockSpec((tm, d), lambda i: (i, 0)),
            pl.BlockSpec(w.shape, lambda i: (0, 0), pipeline_mode=pl.Buffered(1)),
            pl.BlockSpec((tm, LANES), lambda i: (i, 0)),
            pl.BlockSpec((tm, LANES), lambda i: (i, 0)),
        ],
        out_specs=pl.BlockSpec((3, tm, n), lambda i: (0, i, 0)),
        out_shape=jax.ShapeDtypeStruct((3, m, n), F32),
        compiler_params=_cparams(("parallel",), 40),
        name="qkv_rope",
    )(x, w, cos_t, sin_t)


def _rwkv_proj_body(x_ref, w_ref, mu_ref, w0_ref, w2_ref, a0_ref, a2_ref,
                    rkv_ref, lw_ref, ao_ref, go_ref, carry_ref, *, tiles_per_seq):
    tm = x_ref.shape[0]
    n = RWKV_DIM
    row0 = lax.broadcasted_iota(jnp.int32, (tm, 1), 0) == 0

    @pl.when((pl.program_id(0) % tiles_per_seq) == 0)
    def _():
        carry_ref[...] = jnp.zeros_like(carry_ref)

    xb = x_ref[...].astype(BF16)

    def shifted(lo, hi):
        z = jnp.dot(xb, w_ref[:, lo:hi], preferred_element_type=F32)
        prev = jnp.where(row0, carry_ref[0:1, lo:hi], pltpu.roll(z, 1, axis=0))
        carry_ref[0:1, lo:hi] = z[tm - 1:tm, :]
        return z + (prev - z) * mu_ref[:, lo:hi]

    for which in range(3):
        rkv_ref[which] = shifted(which * n, (which + 1) * n)
    ls = shifted(3 * n, 3 * n + LORA_PAD)
    w_lo = ls[:, 0:LANES]
    a_lo = ls[:, LANES:2 * LANES]
    go_ref[...] = ls[:, 2 * LANES:]
    w = w0_ref[...] + jnp.dot(jnp.tanh(w_lo).astype(BF16), w2_ref[...], preferred_element_type=F32)
    x = -w
    softplus = jnp.maximum(x, 0.0) + jnp.log(1.0 + jnp.exp(-jnp.abs(x)))
    lw_ref[...] = -jnp.exp(-softplus - 0.5)
    a = a0_ref[...] + jnp.dot(a_lo.astype(BF16), a2_ref[...], preferred_element_type=F32)
    ao_ref[...] = jax.nn.sigmoid(a)


def _rwkv_proj(x, w_all, mu_all, w0, w2p, a0, a2p, *, seq, tm=256):
    m, d = x.shape
    n = RWKV_DIM
    const = lambda a: pl.BlockSpec(a.shape, lambda i: (0, 0), pipeline_mode=pl.Buffered(1))
    return pl.pallas_call(
        functools.partial(_rwkv_proj_body, tiles_per_seq=seq // tm),
        grid=(m // tm,),
        in_specs=[pl.BlockSpec((tm, d), lambda i: (i, 0)),
                  const(w_all), const(mu_all), const(w0), const(w2p), const(a0), const(a2p)],
        out_specs=[pl.BlockSpec((3, tm, n), lambda i: (0, i, 0))]
        + [pl.BlockSpec((tm, n), lambda i: (i, 0))] * 2
        + [pl.BlockSpec((tm, GATE_LORA), lambda i: (i, 0))],
        out_shape=[jax.ShapeDtypeStruct((3, m, n), F32)]
        + [jax.ShapeDtypeStruct((m, n), F32)] * 2
        + [jax.ShapeDtypeStruct((m, GATE_LORA), F32)],
        scratch_shapes=[pltpu.VMEM((8, w_all.shape[1]), F32)],
        compiler_params=_cparams(("arbitrary",), 48),
        name="rwkv_proj",
    )(x, w_all, mu_all, w0, w2p, a0, a2p)


_NN = (((1,), (0,)), ((), ()))
_NT = (((1,), (1,)), ((), ()))
_TN = (((0,), (0,)), ((), ()))


def _split(a):
    hi = a.astype(BF16)
    return hi, (a - hi.astype(F32)).astype(BF16)


def _dot(a, b, dims=_NN):
    return lax.dot_general(a, b, dims, preferred_element_type=F32)


def _mm3(a, b, dims=_NN):
    ah, al = _split(a)
    bh, bl = _split(b)
    return _dot(ah, bh, dims) + (_dot(ah, bl, dims) + _dot(al, bh, dims))


def _cumsum_rows(tri, x):
    hi, lo = _split(x)
    lo2 = (x - hi.astype(F32) - lo.astype(F32)).astype(BF16)
    return _dot(tri, hi) + (_dot(tri, lo) + _dot(tri, lo2))


_MK_M0, _MK_M1, _MK_TL, _MK_BR, _MK_EYE, _MK_HEAD, _MK_LVL = 0, 1, 2, 3, 4, 5, 6


def _wkv_masks():
    C = CHUNK
    ri, ci = np.meshgrid(np.arange(2 * C), np.arange(2 * C), indexing="ij")
    ti, sj = ri % C, ci % C
    top, left = ri < C, ci < C
    masks = [
        (sj < ti) | (~top & (sj == ti)),
        (sj < ti) | (top & (sj == ti)),
        top & left,
        ~top & ~left,
        ri == ci,
        (ri < HEAD_DIM) == (ci < HEAD_DIM),
    ]
    s = 1
    while s < C:
        masks.append((ri // (2 * s) == ci // (2 * s)) & ((ri // s) % 2 == 1) & ((ci // s) % 2 == 0))
        s *= 2
    tri = np.tril(np.ones((C, C), np.float32))
    return jnp.asarray(np.stack(masks).astype(np.float32)), jnp.asarray(tri, dtype=BF16)


def _wkv_body(r_ref, lw_ref, k_ref, v_ref, a_ref, kk_ref, ka_ref, rk_ref, gg_ref, gb_ref,
              mk_ref, tri_ref, *refs, cast_blocks):
    ncast = len(cast_blocks)
    cast_in, y_ref, cast_out = refs[:ncast], refs[ncast], refs[ncast + 1:2 * ncast + 1]
    g_ref, rp_ref, y0_ref, tm_ref, g0_ref = refs[2 * ncast + 1:]
    step = pl.program_id(0) * pl.num_programs(1) + pl.program_id(1)
    for src, dst, nblk in zip(cast_in, cast_out, cast_blocks):
        @pl.when(step < nblk)
        def _():
            dst[...] = src[...].astype(BF16)

    C = CHUNK
    tc = r_ref.shape[1]
    nch = tc // C
    seqs = [(b, slice(q * LANES, (q + 1) * LANES))
            for b in range(r_ref.shape[0]) for q in range(r_ref.shape[2] // LANES)]
    nb = len(seqs)

    @pl.when(pl.program_id(1) == 0)
    def _():
        g_ref[...] = jnp.zeros_like(g_ref)

    lane = lax.broadcasted_iota(jnp.int32, (1, LANES), 1)
    h0 = lane < HEAD_DIM
    m0 = jnp.where(h0, 1.0, 0.0)
    m1 = 1.0 - m0
    zeros_c = jnp.zeros((C, LANES), BF16)
    tri = tri_ref[...]
    mk = lambda i: mk_ref[i]

    def segsum(x):
        s0 = jnp.sum(x * m0, axis=-1, keepdims=True)
        s1 = jnp.sum(x * m1, axis=-1, keepdims=True)
        return jnp.where(h0, s0, s1)

    b16 = lambda x: x.astype(BF16)
    cat0 = lambda *xs: jnp.concatenate(xs, 0)
    cat1 = lambda *xs: jnp.concatenate(xs, 1)

    each = lambda f, *ls: [f(*xs) for xs in zip(*ls)]

    def y_rhs(ap, xw, x0, x1):
        return cat0(cat1(b16(ap * m0), b16(xw * m0)), cat1(zeros_c, x0), cat1(zeros_c, x1),
                    cat1(b16(ap * m1), b16(xw * m1)))

    def prep(s, c):
        b, ln = seqs[s]
        rows = slice(c * C, (c + 1) * C)
        r, ks, v, a = r_ref[b, rows, ln], k_ref[b, rows, ln], v_ref[b, rows, ln], a_ref[b, rows, ln]
        lw = lw_ref[b, rows, ln]
        kk = ks * kk_ref[:, ln]
        kk = kk / jnp.maximum(jnp.sqrt(segsum(kk * kk)), 1e-12)
        kmod = ks * (1.0 + (a - 1.0) * ka_ref[:, ln])
        y_ref[b, rows, ln] = segsum(r * kmod * rk_ref[:, ln]) * v
        bv = kk * a
        cs = _cumsum_rows(tri, lw)
        cl = cs[C - 1:C, :]
        At = -kk * jnp.exp(cs - lw)
        Rt = r * jnp.exp(cs)
        pinv = jnp.exp(-cs)
        pend = jnp.exp(cl - cs)
        return dict(s=s, c=c, Rt=Rt, cl=cl, Bt=b16(bv * pinv), Kt=b16(kmod * pinv),
                    BKh=b16(cat0(bv * pend, kmod * pend)), At0=b16(At * m0), At1=b16(At * m1),
                    Rt0=b16(Rt * m0), Rt1=b16(Rt * m1), v0=b16(v * m0), v1=b16(v * m1), vb=b16(v))

    def stages(P):
        get = lambda k: [p[k] for p in P]
        At0, At1, v0, v1 = get("At0"), get("At1"), get("v0"), get("v1")
        M0 = each(lambda p: _dot(cat0(p["At0"], p["Rt0"]), cat0(p["Bt"], p["Kt"]), _NT), P)
        yield
        M1 = each(lambda p: _dot(cat0(p["Rt1"], p["At1"]), cat0(p["Kt"], p["Bt"]), _NT), P)
        yield
        M0 = each(lambda x: jnp.where(mk(_MK_M0) > 0.5, x, 0.0), M0)
        M1 = each(lambda x: jnp.where(mk(_MK_M1) > 0.5, x, 0.0), M1)
        M0b, M1b = each(b16, M0), each(b16, M1)

        Lbd = each(lambda x0, x1: x0 * mk(_MK_TL) + x1 * mk(_MK_BR), M0, M1)
        X = each(lambda l: mk(_MK_EYE) + l * mk(_MK_LVL), Lbd)
        for lv in range(1, C.bit_length() - 1):
            Xb = each(b16, X)
            XC = each(lambda xb, l: b16(_dot(xb, b16(l * mk(_MK_LVL + lv)))), Xb, Lbd)
            yield
            X = each(lambda x, xc, xb: x + _dot(xc, xb), X, XC, Xb)
            yield
        Xb = each(b16, X)

        W = each(lambda a, b_, x0, x1: _dot(cat1(a[:C], b_[C:]), cat0(zeros_c, x0, x1, zeros_c)),
                 M0b, M1b, v0, v1)
        yield
        res = each(lambda xb, a0, a1, w: _dot(xb, cat0(cat1(a0, b16(w * m0)), cat1(a1, b16(w * m1)))),
                   Xb, At0, At1, W)
        yield
        Ap = each(lambda x: x[:C, :LANES] + x[C:, :LANES], res)
        XW = each(lambda x: x[:C, LANES:] + x[C:, LANES:], res)
        res_y = each(lambda a, b_, ap, xw, x0, x1: _dot(cat1(a[C:], b_[:C]), y_rhs(ap, xw, x0, x1)),
                     M0b, M1b, Ap, XW, v0, v1)
        yield
        res_t = each(lambda ap, xw, p: _dot(cat0(cat1(b16(ap), b16(xw)), cat1(zeros_c, p["vb"])), p["BKh"], _TN),
                     Ap, XW, P)
        for p, ry, rt in zip(P, res_y, res_t):
            s, c = p["s"], p["c"]
            rows = slice(c * C, (c + 1) * C)
            rp_ref[s, rows, :] = p["Rt"] + ry[:, :LANES]
            y0_ref[s, rows, :] = ry[:, LANES:]
            tm_ref[s * nch + c] = (rt[:LANES] + mk(_MK_EYE) * jnp.exp(p["cl"])) * mk(_MK_HEAD)
            g0_ref[s * nch + c] = rt[LANES:] * mk(_MK_HEAD)
        yield

    chains = [(s, c) for c in range(nch) for s in range(nb)]
    per_group = -(-len(chains) // WKV_GROUPS)
    groups = [chains[g0:g0 + per_group] for g0 in range(0, len(chains), per_group)]
    P = [prep(*ch) for ch in groups[0]]
    for gi in range(len(groups)):
        todo = list(groups[gi + 1]) if gi + 1 < len(groups) else []
        nxt = []
        for _ in stages(P):
            if todo:
                nxt.append(prep(*todo.pop(0)))
        nxt += [prep(*ch) for ch in todo]
        P = nxt

    Gs = [g_ref[b] for b in range(nb)]
    for c in range(nch):
        rows = slice(c * C, (c + 1) * C)
        for b in range(nb):
            y0_ref[b, rows, :] = _dot(b16(rp_ref[b, rows, :]), b16(Gs[b]), _NT) + y0_ref[b, rows, :]
        Gs = [_mm3(Gs[b], tm_ref[b * nch + c]) + g0_ref[b * nch + c] for b in range(nb)]
    for b in range(nb):
        g_ref[b] = Gs[b]

    for s, (b, ln) in enumerate(seqs):
        y = y0_ref[s]
        yc = y - segsum(y) * (1.0 / HEAD_DIM)
        var = segsum(yc * yc) * (1.0 / HEAD_DIM)
        y_ref[b, :, ln] = y_ref[b, :, ln] + (yc * lax.rsqrt(var + GN_EPS) * gg_ref[:, ln] + gb_ref[:, ln])


def _cast_row_blocks(rows, nsteps):
    return max(nb for nb in range(1, nsteps + 1) if rows % nb == 0 and (rows // nb) % BF16_SUBLANES == 0)


def _wkv(rkv, lw, a, k_k, k_a, r_k, gn_g, gn_b, to_bf16=(), *, batch, seq, tc=128, tiles=5):
    n = RWKV_DIM
    shp = (batch, seq, n)
    tc = min(tc, seq)
    nch = tc // CHUNK
    nseq = batch * tiles
    wl = tiles * LANES
    masks, tri = _wkv_masks()
    seq_spec = pl.BlockSpec((batch, tc, wl), lambda p, i: (0, i, p))
    rkv_spec = lambda which: pl.BlockSpec((None, batch, tc, wl), lambda p, i: (which, 0, i, p))
    par_spec = pl.BlockSpec((1, wl), lambda p, i: (0, p))
    rkv = rkv.reshape((3,) + shp)
    grid = (n // wl, seq // tc)
    cast_blocks = [_cast_row_blocks(w.shape[0], grid[0] * grid[1]) for w in to_bf16]

    def cast_spec(w, nblk):
        return pl.BlockSpec((w.shape[0] // nblk, w.shape[1]),
                            lambda p, i: (jnp.minimum(p * grid[1] + i, nblk - 1), 0))

    cast_specs = [cast_spec(w, nblk) for w, nblk in zip(to_bf16, cast_blocks)]
    out, *casted = pl.pallas_call(
        functools.partial(_wkv_body, cast_blocks=tuple(cast_blocks)),
        grid=grid,
        in_specs=[rkv_spec(0), seq_spec, rkv_spec(1), rkv_spec(2), seq_spec] + [par_spec] * 5 + [
            pl.BlockSpec(masks.shape, lambda p, i: (0, 0, 0)),
            pl.BlockSpec(tri.shape, lambda p, i: (0, 0)),
        ] + cast_specs,
        out_specs=[seq_spec] + cast_specs,
        out_shape=[jax.ShapeDtypeStruct(shp, F32)] + [jax.ShapeDtypeStruct(w.shape, BF16) for w in to_bf16],
        scratch_shapes=[
            pltpu.VMEM((nseq, LANES, LANES), F32),
            pltpu.VMEM((nseq, tc, LANES), F32),
            pltpu.VMEM((nseq, tc, LANES), F32),
            pltpu.VMEM((nseq * nch, LANES, LANES), F32),
            pltpu.VMEM((nseq * nch, LANES, LANES), F32),
        ],
        compiler_params=_cparams(("parallel", "arbitrary"), 32),
        name="wkv",
    )(rkv, lw.reshape(shp), rkv, rkv, a.reshape(shp), k_k, k_a, r_k, gn_g, gn_b, masks, tri, *to_bf16)
    return out.reshape(batch * seq, n), casted


ATTN_TILE = ATTN_BLOCK * max(DILATIONS)
ATTN_GROUP = 4
LOG2E = 1.4426950408889634


def _attn_bias():
    qi, kj = np.meshgrid(np.arange(ATTN_BLOCK), np.arange(2 * ATTN_BLOCK), indexing="ij")
    band = (kj >= qi) & (kj <= qi + ATTN_BLOCK)
    return jnp.asarray(np.stack([np.where(band, 0.0, NEG_BIG), np.where(kj >= ATTN_BLOCK, 0.0, NEG_BIG)]), F32)


def _dil_attn_body(q_ref, kc_ref, vc_ref, kp_ref, vp_ref, bias_ref, o_ref, m_ref, l_ref):
    blk = ATTN_BLOCK
    tile = q_ref.shape[0]
    first = jnp.where(pl.program_id(2) == 0, 1.0, 0.0)
    bias_first = bias_ref[0] + first * bias_ref[1]

    lane = lax.broadcasted_iota(jnp.int32, (1, LANES), 1)
    h0 = lane < HEAD_DIM

    def rows(d, rho, n):
        start = rho + d * blk * n
        return slice(start, start + blk) if d == 1 else pl.ds(start, blk, stride=d)

    def scores(d, nblk, rho, n):
        q = q_ref[rows(d, rho, n), :].astype(BF16)
        if n == 0:
            kp, bias = kp_ref[rows(d, rho, nblk - 1), :], bias_first
        else:
            kp, bias = kc_ref[rows(d, rho, n - 1), :], bias_ref[0]
        keys = jnp.concatenate([kp, kc_ref[rows(d, rho, n), :]], 0).astype(BF16)
        zero = jnp.zeros_like(q)
        return [lax.dot_general(jnp.where(hm, q, zero), keys, _NT, preferred_element_type=F32) + bias
                for hm in (h0, jnp.logical_not(h0))]

    def softmax(ss):
        ms = [jnp.max(s, axis=-1, keepdims=True) for s in ss]
        ps = [jnp.exp2(s - m) for s, m in zip(ss, ms)]
        ls = [jnp.sum(p, axis=-1, keepdims=True) for p in ps]
        return (jnp.concatenate([p.astype(BF16) for p in ps], 1),
                jnp.where(h0, ms[0], ms[1]), jnp.where(h0, ls[0], ls[1]))

    def weighted(d, nblk, rho, n, p):
        vp = vp_ref[rows(d, rho, nblk - 1), :] if n == 0 else vc_ref[rows(d, rho, n - 1), :]
        vals = jnp.concatenate([vp, vc_ref[rows(d, rho, n), :]], 0).astype(BF16)
        zv = jnp.zeros_like(vals)
        vbd = jnp.concatenate([jnp.where(h0, vals, zv), jnp.where(h0, zv, vals)], 0)
        return jnp.dot(p, vbd, preferred_element_type=F32)

    for g, d in enumerate(DILATIONS):
        nblk = tile // (blk * d)
        items = [(rho, n) for rho in range(d) for n in range(nblk)]
        for i0 in range(0, len(items), ATTN_GROUP):
            grp = items[i0:i0 + ATTN_GROUP]
            ss = [scores(d, nblk, rho, n) for rho, n in grp]
            sm = [softmax(s) for s in ss]
            os_ = [weighted(d, nblk, rho, n, p) for (rho, n), (p, _, _) in zip(grp, sm)]
            for (rho, n), (_, mb, lb), o in zip(grp, sm, os_):
                out_rows = rows(d, rho, n)
                if g == 0:
                    o_ref[out_rows, :] = o
                    m_ref[out_rows, :] = mb
                    l_ref[out_rows, :] = lb
                else:
                    m_old = m_ref[out_rows, :]
                    m_new = jnp.maximum(m_old, mb)
                    a_old = jnp.exp2(m_old - m_new)
                    a_blk = jnp.exp2(mb - m_new)
                    o_ref[out_rows, :] = o_ref[out_rows, :] * a_old + o * a_blk
                    l_ref[out_rows, :] = l_ref[out_rows, :] * a_old + lb * a_blk
                    m_ref[out_rows, :] = m_new
    o_ref[...] = o_ref[...] / l_ref[...]


def _dil_attn(qkv, *, batch, seq):
    assert seq % ATTN_TILE == 0
    x = qkv.reshape(3, batch, seq, ATTN_DIM)
    bias = _attn_bias()

    def cur(which):
        return pl.BlockSpec((None, None, ATTN_TILE, LANES), lambda b, p, i: (which, b, i, p))

    def prev(which):
        return pl.BlockSpec((None, None, ATTN_TILE, LANES), lambda b, p, i: (which, b, jnp.maximum(i - 1, 0), p))

    o = pl.pallas_call(
        _dil_attn_body,
        grid=(batch, ATTN_DIM // LANES, seq // ATTN_TILE),
        in_specs=[cur(0), cur(1), cur(2), prev(1), prev(2),
                  pl.BlockSpec(bias.shape, lambda b, p, i: (0, 0, 0))],
        out_specs=pl.BlockSpec((None, ATTN_TILE, LANES), lambda b, p, i: (b, i, p)),
        out_shape=jax.ShapeDtypeStruct((batch, seq, ATTN_DIM), F32),
        scratch_shapes=[pltpu.VMEM((ATTN_TILE, LANES), F32)] * 2,
        compiler_params=_cparams(("parallel", "parallel", "arbitrary"), 40),
        name="dil_attn",
    )(x, x, x, x, x, bias)
    return o.reshape(batch * seq, ATTN_DIM)


def _out_ln_body(yw_ref, gl_ref, ya_ref, h_ref, g2_ref, wr_ref, wa_ref, g_ref, b_ref, out_ref):
    gate = jnp.dot(jax.nn.sigmoid(gl_ref[...]).astype(BF16), g2_ref[...], preferred_element_type=F32)
    yr = yw_ref[...] * gate
    mix = (jnp.dot(yr.astype(BF16), wr_ref[...], preferred_element_type=F32)
           + jnp.dot(ya_ref[...].astype(BF16), wa_ref[...], preferred_element_type=F32))
    z = ALPHA * h_ref[...] + mix
    out_ref[...] = _layer_norm(z, g_ref[...], b_ref[...])


def _out_ln(yw, glo, ya, h, g2, wo_r, wo_a, g, b, *, tm=512):
    m, d = h.shape
    rows = lambda w: pl.BlockSpec((tm, w), lambda i: (i, 0))
    full = lambda a: pl.BlockSpec(a.shape, lambda i: (0, 0))
    return pl.pallas_call(
        _out_ln_body,
        grid=(m // tm,),
        in_specs=[rows(RWKV_DIM), rows(GATE_LORA), rows(ATTN_DIM), rows(d),
                  full(g2), full(wo_r), full(wo_a), full(g), full(b)],
        out_specs=rows(d),
        out_shape=jax.ShapeDtypeStruct((m, d), F32),
        compiler_params=_cparams(("parallel",), 48),
        name="out_ln",
    )(yw, glo, ya, h, g2, wo_r, wo_a, g, b)


def _pad_rows(w, rows):
    return jnp.pad(w, ((0, rows - w.shape[0]), (0, 0)))


def _pad_cols(w, cols):
    return jnp.pad(w, ((0, 0), (0, cols - w.shape[1])))


def kernel(x, positions, ffn1_w_gate, ffn1_w_up, ffn1_w_down, ln1_g, ln1_b, w_in, mu_r, mu_k, mu_v, mu_w, mu_a, mu_g, w0, w2, a0, a2, g2, k_k, k_a, r_k, gn_g, gn_b, w_out, ln2_g, ln2_b, ffn2_w_gate, ffn2_w_up, ffn2_w_down, ln3_g, ln3_b):
    batch, seq, d = x.shape
    m = batch * seq
    h = x.reshape(m, d)
    cos_t, sin_t = _rope_tables(positions.astype(F32).reshape(m, 1))
    c = RWKV_DIM
    for l in range(DEPTH):
        row = lambda p: p[l].reshape(1, -1)
        h = _ffn_ln(h, ffn1_w_gate[l].astype(BF16), ffn1_w_up[l].astype(BF16),
                    ffn1_w_down[l].astype(BF16), row(ln1_g), row(ln1_b))

        wi = w_in[l]
        o_lo = 3 * c
        o_q = o_lo + DECAY_LORA + ICLR_LORA + GATE_LORA
        w_rwkv = jnp.concatenate([
            wi[:, :o_lo],
            _pad_cols(wi[:, o_lo:o_lo + DECAY_LORA], LANES),
            _pad_cols(wi[:, o_lo + DECAY_LORA:o_lo + DECAY_LORA + ICLR_LORA], LANES),
            wi[:, o_lo + DECAY_LORA + ICLR_LORA:o_q]], axis=1).astype(BF16)
        mu_rwkv = jnp.concatenate([row(mu_r), row(mu_k), row(mu_v), _pad_cols(row(mu_w), LANES),
                                   _pad_cols(row(mu_a), LANES), row(mu_g)], axis=1)
        w_qkv = wi[:, o_q:].astype(BF16)

        qkv = _qkv_rope(h, w_qkv, cos_t, sin_t)
        rkv_s, lw, a_s, glo = _rwkv_proj(h, w_rwkv, mu_rwkv, row(w0), _pad_rows(w2[l], LANES).astype(BF16),
                                         row(a0), _pad_rows(a2[l], LANES).astype(BF16), seq=seq)
        yw, ffn2_w = _wkv(rkv_s, lw, a_s, row(k_k), row(k_a), row(r_k), row(gn_g), row(gn_b),
                          (ffn2_w_gate[l], ffn2_w_up[l], ffn2_w_down[l]), batch=batch, seq=seq)
        ya = _dil_attn(qkv, batch=batch, seq=seq)

        wo = w_out[l].astype(BF16)
        h = _out_ln(yw, glo, ya, h, g2[l].astype(BF16), wo[:c], wo[c:], row(ln2_g), row(ln2_b))

        h = _ffn_ln(h, *ffn2_w, row(ln3_g), row(ln3_b))
    return h.reshape(batch, seq, d)
```

```python
import functools

import numpy as np

import jax
import jax.numpy as jnp
from jax import lax
from jax.experimental import pallas as pl
from jax.experimental.pallas import tpu as pltpu

F32 = jnp.float32
BF16 = jnp.bfloat16

D_MODEL = 2048
HEAD_DIM = 64
ATTN_HEADS = 12
RWKV_HEADS = 20
ATTN_DIM = ATTN_HEADS * HEAD_DIM
RWKV_DIM = RWKV_HEADS * HEAD_DIM
DECAY_LORA = 96
ICLR_LORA = 96
GATE_LORA = 256
DILATIONS = (1, 4, 16)
ATTN_BLOCK = 128
ROPE_THETA = 500000.0
ROT_DIM = HEAD_DIM // 4
D_FF = 5632
DEPTH = 1
ALPHA = (2 * DEPTH) ** 0.25
LN_EPS = 1e-5
GN_EPS = 64e-5

LANES = 128
SUBLANES = 8
BF16_SUBLANES = 16
LORA_PAD = 512
CHUNK = 64
WKV_GROUPS = 4
NEG_BIG = -1e30
HI = lax.Precision.HIGHEST


def _cparams(sem, vmem_mb):
    return pltpu.CompilerParams(dimension_semantics=sem, vmem_limit_bytes=vmem_mb * 1024 * 1024)


def _layer_norm(z, g, b):
    mu = jnp.mean(z, axis=-1, keepdims=True)
    zc = z - mu
    var = jnp.mean(zc * zc, axis=-1, keepdims=True)
    return zc * lax.rsqrt(var + LN_EPS) * g + b


def _ffn_ln_body(x_ref, wg_ref, wu_ref, wd_ref, g_ref, b_ref, o_ref, xb_ref, *, nf):
    j = pl.program_id(1)

    @pl.when(j == 0)
    def _():
        xb_ref[...] = x_ref[...].astype(BF16)
        o_ref[...] = jnp.zeros_like(o_ref)

    xb = xb_ref[...]
    gate = jnp.dot(xb, wg_ref[...], preferred_element_type=F32)
    up = jnp.dot(xb, wu_ref[...], preferred_element_type=F32)
    hmid = (gate * jax.nn.sigmoid(gate)) * up
    o_ref[...] += jnp.dot(hmid.astype(BF16), wd_ref[...], preferred_element_type=F32)

    @pl.when(j == nf - 1)
    def _():
        z = ALPHA * x_ref[...] + 0.5 * o_ref[...]
        o_ref[...] = _layer_norm(z, g_ref[...], b_ref[...])


def _ffn_ln(x, wg, wu, wd, g, b, *, tm=512, tf=512):
    m, d = x.shape
    dff = wg.shape[1]
    nf = dff // tf
    return pl.pallas_call(
        functools.partial(_ffn_ln_body, nf=nf),
        grid=(m // tm, nf),
        in_specs=[
            pl.BlockSpec((tm, d), lambda i, j: (i, 0)),
            pl.BlockSpec((d, tf), lambda i, j: (0, j)),
            pl.BlockSpec((d, tf), lambda i, j: (0, j)),
            pl.BlockSpec((tf, d), lambda i, j: (j, 0)),
            pl.BlockSpec((1, d), lambda i, j: (0, 0)),
            pl.BlockSpec((1, d), lambda i, j: (0, 0)),
        ],
        out_specs=pl.BlockSpec((tm, d), lambda i, j: (i, 0)),
        out_shape=jax.ShapeDtypeStruct((m, d), F32),
        scratch_shapes=[pltpu.VMEM((tm, d), BF16)],
        compiler_params=_cparams(("parallel", "arbitrary"), 40),
        name="ffn_ln",
    )(x, wg, wu, wd, g, b)


def _qkv_rope_body(x_ref, w_ref, pos_ref, invf_ref, o_ref):
    xb = x_ref[...].astype(BF16)
    l64 = lax.broadcasted_iota(jnp.int32, (1, LANES), 1) & (HEAD_DIM - 1)
    lo = l64 < ROT_DIM // 2
    hi = l64 < ROT_DIM
    ang = pos_ref[...] * invf_ref[...]
    c = jnp.where(hi, jnp.cos(ang), 1.0)
    s = jnp.where(hi, jnp.sin(ang), 0.0)
    for which in range(3):
        y = jnp.dot(xb, w_ref[:, which * ATTN_DIM:(which + 1) * ATTN_DIM], preferred_element_type=F32)
        if which == 2:
            o_ref[which] = y
            continue
        for t in range(ATTN_DIM // LANES):
            yt = y[:, t * LANES:(t + 1) * LANES]
            up = pltpu.roll(yt, LANES - ROT_DIM // 2, axis=1)
            dn = pltpu.roll(yt, ROT_DIM // 2, axis=1)
            rot = jnp.where(lo, -up, jnp.where(hi, dn, 0.0))
            yt = yt * c + rot * s
            if which == 0:
                yt = yt * (HEAD_DIM ** -0.5 * LOG2E)
            o_ref[which, :, t * LANES:(t + 1) * LANES] = yt


def _qkv_rope(x, w, pos_f32, *, tm=512):
    m, d = x.shape
    n = w.shape[1] // 3
    half = ROT_DIM // 2
    inv_freq = jnp.power(ROPE_THETA, -jnp.arange(half, dtype=F32) * (2.0 / ROT_DIM))
    l64 = jnp.arange(LANES) % HEAD_DIM
    invf = jnp.where(l64 < ROT_DIM, inv_freq[l64 % half], 0.0).astype(F32)[None, :]
    return pl.pallas_call(
        _qkv_rope_body,
        grid=(m // tm,),
        in_specs=[
            pl.BlockSpec((tm, d), lambda i: (i, 0)),
            pl.BlockSpec(w.shape, lambda i: (0, 0), pipeline_mode=pl.Buffered(1)),
            pl.BlockSpec((tm, 1), lambda i: (i, 0)),
            pl.BlockSpec((1, LANES), lambda i: (0, 0)),
        ],
        out_specs=pl.BlockSpec((3, tm, n), lambda i: (0, i, 0)),
        out_shape=jax.ShapeDtypeStruct((3, m, n), F32),
        compiler_params=_cparams(("parallel",), 40),
        name="qkv_rope",
    )(x, w, pos_f32, invf)


def _rwkv_proj_body(x_ref, w_ref, mu_ref, w0_ref, w2_ref, a0_ref, a2_ref,
                    rkv_ref, lw_ref, ao_ref, go_ref, carry_ref, *, tiles_per_seq):
    tm = x_ref.shape[0]
    n = RWKV_DIM
    row0 = lax.broadcasted_iota(jnp.int32, (tm, 1), 0) == 0

    @pl.when((pl.program_id(0) % tiles_per_seq) == 0)
    def _():
        carry_ref[...] = jnp.zeros_like(carry_ref)

    xb = x_ref[...].astype(BF16)

    def shifted(lo, hi):
        z = jnp.dot(xb, w_ref[:, lo:hi], preferred_element_type=F32)
        prev = jnp.where(row0, carry_ref[0:1, lo:hi], pltpu.roll(z, 1, axis=0))
        carry_ref[0:1, lo:hi] = z[tm - 1:tm, :]
        return z + (prev - z) * mu_ref[:, lo:hi]

    for which in range(3):
        rkv_ref[which] = shifted(which * n, (which + 1) * n)
    ls = shifted(3 * n, 3 * n + LORA_PAD)
    w_lo = ls[:, 0:LANES]
    a_lo = ls[:, LANES:2 * LANES]
    go_ref[...] = ls[:, 2 * LANES:]
    w = w0_ref[...] + jnp.dot(jnp.tanh(w_lo).astype(BF16), w2_ref[...], preferred_element_type=F32)
    x = -w
    softplus = jnp.maximum(x, 0.0) + jnp.log(1.0 + jnp.exp(-jnp.abs(x)))
    lw_ref[...] = -jnp.exp(-softplus - 0.5)
    a = a0_ref[...] + jnp.dot(a_lo.astype(BF16), a2_ref[...], preferred_element_type=F32)
    ao_ref[...] = jax.nn.sigmoid(a)


def _rwkv_proj(x, w_all, mu_all, w0, w2p, a0, a2p, *, seq, tm=256):
    m, d = x.shape
    n = RWKV_DIM
    const = lambda a: pl.BlockSpec(a.shape, lambda i: (0, 0), pipeline_mode=pl.Buffered(1))
    return pl.pallas_call(
        functools.partial(_rwkv_proj_body, tiles_per_seq=seq // tm),
        grid=(m // tm,),
        in_specs=[pl.BlockSpec((tm, d), lambda i: (i, 0)),
                  const(w_all), const(mu_all), const(w0), const(w2p), const(a0), const(a2p)],
        out_specs=[pl.BlockSpec((3, tm, n), lambda i: (0, i, 0))]
        + [pl.BlockSpec((tm, n), lambda i: (i, 0))] * 2
        + [pl.BlockSpec((tm, GATE_LORA), lambda i: (i, 0))],
        out_shape=[jax.ShapeDtypeStruct((3, m, n), F32)]
        + [jax.ShapeDtypeStruct((m, n), F32)] * 2
        + [jax.ShapeDtypeStruct((m, GATE_LORA), F32)],
        scratch_shapes=[pltpu.VMEM((8, w_all.shape[1]), F32)],
        compiler_params=_cparams(("arbitrary",), 48),
        name="rwkv_proj",
    )(x, w_all, mu_all, w0, w2p, a0, a2p)


_NN = (((1,), (0,)), ((), ()))
_NT = (((1,), (1,)), ((), ()))
_TN = (((0,), (0,)), ((), ()))


def _split(a):
    hi = a.astype(BF16)
    return hi, (a - hi.astype(F32)).astype(BF16)


def _dot(a, b, dims=_NN):
    return lax.dot_general(a, b, dims, preferred_element_type=F32)


def _mm3(a, b, dims=_NN):
    ah, al = _split(a)
    bh, bl = _split(b)
    return _dot(ah, bh, dims) + (_dot(ah, bl, dims) + _dot(al, bh, dims))


def _cumsum_rows(tri, x):
    hi, lo = _split(x)
    lo2 = (x - hi.astype(F32) - lo.astype(F32)).astype(BF16)
    return _dot(tri, hi) + (_dot(tri, lo) + _dot(tri, lo2))


_MK_M0, _MK_M1, _MK_TL, _MK_BR, _MK_EYE, _MK_HEAD, _MK_LVL = 0, 1, 2, 3, 4, 5, 6


def _wkv_masks():
    C = CHUNK
    ri, ci = np.meshgrid(np.arange(2 * C), np.arange(2 * C), indexing="ij")
    ti, sj = ri % C, ci % C
    top, left = ri < C, ci < C
    masks = [
        (sj < ti) | (~top & (sj == ti)),
        (sj < ti) | (top & (sj == ti)),
        top & left,
        ~top & ~left,
        ri == ci,
        (ri < HEAD_DIM) == (ci < HEAD_DIM),
    ]
    s = 1
    while s < C:
        masks.append((ri // (2 * s) == ci // (2 * s)) & ((ri // s) % 2 == 1) & ((ci // s) % 2 == 0))
        s *= 2
    tri = np.tril(np.ones((C, C), np.float32))
    masks = np.stack(masks).astype(np.float32)
    return jnp.asarray(masks), jnp.asarray(masks[_MK_LVL:], dtype=BF16), jnp.asarray(tri, dtype=BF16)


def _wkv_body(r_ref, lw_ref, k_ref, v_ref, a_ref, kk_ref, ka_ref, rk_ref, gg_ref, gb_ref,
              mk_ref, mk16_ref, tri_ref, *refs, cast_blocks):
    ncast = len(cast_blocks)
    cast_in, y_ref, cast_out = refs[:ncast], refs[ncast], refs[ncast + 1:2 * ncast + 1]
    g_ref, rp_ref, y0_ref, tm_ref, g0_ref = refs[2 * ncast + 1:]
    step = pl.program_id(0) * pl.num_programs(1) + pl.program_id(1)
    for src, dst, nblk in zip(cast_in, cast_out, cast_blocks):
        @pl.when(step < nblk)
        def _():
            dst[...] = src[...].astype(BF16)

    C = CHUNK
    tc = r_ref.shape[1]
    nch = tc // C
    seqs = [(b, slice(q * LANES, (q + 1) * LANES))
            for b in range(r_ref.shape[0]) for q in range(r_ref.shape[2] // LANES)]
    nb = len(seqs)

    @pl.when(pl.program_id(1) == 0)
    def _():
        g_ref[...] = jnp.zeros_like(g_ref)

    lane = lax.broadcasted_iota(jnp.int32, (1, LANES), 1)
    h0 = lane < HEAD_DIM
    m0 = jnp.where(h0, 1.0, 0.0)
    m1 = 1.0 - m0
    zeros_c = jnp.zeros((C, LANES), BF16)
    tri = tri_ref[...]
    mk = lambda i: mk_ref[i]

    def segsum(x):
        s0 = jnp.sum(x * m0, axis=-1, keepdims=True)
        s1 = jnp.sum(x * m1, axis=-1, keepdims=True)
        return jnp.where(h0, s0, s1)

    b16 = lambda x: x.astype(BF16)
    cat0 = lambda *xs: jnp.concatenate(xs, 0)
    cat1 = lambda *xs: jnp.concatenate(xs, 1)

    each = lambda f, *ls: [f(*xs) for xs in zip(*ls)]

    def y_rhs(ap, xw, x0, x1):
        return cat0(cat1(b16(ap * m0), b16(xw * m0)), cat1(zeros_c, x0), cat1(zeros_c, x1),
                    cat1(b16(ap * m1), b16(xw * m1)))

    def prep(s, c):
        b, ln = seqs[s]
        rows = slice(c * C, (c + 1) * C)
        r, ks, v, a = r_ref[b, rows, ln], k_ref[b, rows, ln], v_ref[b, rows, ln], a_ref[b, rows, ln]
        lw = lw_ref[b, rows, ln]
        kk = ks * kk_ref[:, ln]
        kk = kk / jnp.maximum(jnp.sqrt(segsum(kk * kk)), 1e-12)
        kmod = ks * (1.0 + (a - 1.0) * ka_ref[:, ln])
        y_ref[b, rows, ln] = segsum(r * kmod * rk_ref[:, ln]) * v
        bv = kk * a
        cs = _cumsum_rows(tri, lw)
        cl = cs[C - 1:C, :]
        At = -kk * jnp.exp(cs - lw)
        Rt = r * jnp.exp(cs)
        pinv = jnp.exp(-cs)
        pend = jnp.exp(cl - cs)
        return dict(s=s, c=c, Rt=Rt, cl=cl, Bt=b16(bv * pinv), Kt=b16(kmod * pinv),
                    BKh=b16(cat0(bv * pend, kmod * pend)), At0=b16(At * m0), At1=b16(At * m1),
                    Rt0=b16(Rt * m0), Rt1=b16(Rt * m1), v0=b16(v * m0), v1=b16(v * m1), vb=b16(v))

    def stages(P):
        get = lambda k: [p[k] for p in P]
        At0, At1, v0, v1 = get("At0"), get("At1"), get("v0"), get("v1")
        M0 = each(lambda p: _dot(cat0(p["At0"], p["Rt0"]), cat0(p["Bt"], p["Kt"]), _NT), P)
        yield
        M1 = each(lambda p: _dot(cat0(p["Rt1"], p["At1"]), cat0(p["Kt"], p["Bt"]), _NT), P)
        yield
        M0 = each(lambda x: x * mk(_MK_M0), M0)
        M1 = each(lambda x: x * mk(_MK_M1), M1)
        M0b, M1b = each(b16, M0), each(b16, M1)

        Lbd = each(lambda x0, x1: x0 * mk(_MK_TL) + x1 * mk(_MK_BR), M0, M1)
        Lbd16 = each(b16, Lbd)
        X = each(lambda l: mk(_MK_EYE) + l * mk(_MK_LVL), Lbd)
        for lv in range(1, C.bit_length() - 1):
            s = 1 << lv
            Xb = each(b16, X)
            Cl = each(lambda l: l * mk16_ref[lv], Lbd16)
            if s < SUBLANES:
                XC = each(lambda xb, cl: b16(_dot(xb, cl)), Xb, Cl)
                yield
                X = each(lambda x, xc, xb: x + _dot(xc, xb), X, XC, Xb)
            else:
                odd = lambda x: cat0(*[x[i:i + s] for i in range(s, 2 * C, 2 * s)])
                XC = each(lambda x, cl: b16(_dot(b16(odd(x)), cl)), X, Cl)
                yield
                upd = each(lambda xc, xb: _dot(xc, xb), XC, Xb)
                X = each(lambda x, u: cat0(*[piece for bi, i in enumerate(range(0, 2 * C, 2 * s))
                                             for piece in (x[i:i + s], x[i + s:i + 2 * s] + u[bi * s:(bi + 1) * s])]),
                         X, upd)
            yield
        Xb = each(b16, X)

        W = each(lambda a, b_, x0, x1: _dot(cat1(a[:C], b_[C:]), cat0(zeros_c, x0, x1, zeros_c)),
                 M0b, M1b, v0, v1)
        yield
        res = each(lambda xb, a0, a1, w: _dot(xb, cat0(cat1(a0, b16(w * m0)), cat1(a1, b16(w * m1)))),
                   Xb, At0, At1, W)
        yield
        Ap = each(lambda x: x[:C, :LANES] + x[C:, :LANES], res)
        XW = each(lambda x: x[:C, LANES:] + x[C:, LANES:], res)
        res_y = each(lambda a, b_, ap, xw, x0, x1: _dot(cat1(a[C:], b_[:C]), y_rhs(ap, xw, x0, x1)),
                     M0b, M1b, Ap, XW, v0, v1)
        yield
        res_t = each(lambda ap, xw, p: _dot(cat0(cat1(b16(ap), b16(xw)), cat1(zeros_c, p["vb"])), p["BKh"], _TN),
                     Ap, XW, P)
        for p, ry, rt in zip(P, res_y, res_t):
            s, c = p["s"], p["c"]
            rows = slice(c * C, (c + 1) * C)
            rp_ref[s, rows, :] = p["Rt"] + ry[:, :LANES]
            y0_ref[s, rows, :] = ry[:, LANES:]
            tm_ref[s * nch + c] = (rt[:LANES] + mk(_MK_EYE) * jnp.exp(p["cl"])) * mk(_MK_HEAD)
            g0_ref[s * nch + c] = rt[LANES:] * mk(_MK_HEAD)
        yield

    chains = [(s, c) for c in range(nch) for s in range(nb)]
    per_group = -(-len(chains) // WKV_GROUPS)
    groups = [chains[g0:g0 + per_group] for g0 in range(0, len(chains), per_group)]
    P = [prep(*ch) for ch in groups[0]]
    for gi in range(len(groups)):
        todo = list(groups[gi + 1]) if gi + 1 < len(groups) else []
        nxt = []
        for _ in stages(P):
            if todo:
                nxt.append(prep(*todo.pop(0)))
        nxt += [prep(*ch) for ch in todo]
        P = nxt

    Gs = [g_ref[b] for b in range(nb)]
    for c in range(nch):
        rows = slice(c * C, (c + 1) * C)
        for b in range(nb):
            y0_ref[b, rows, :] = _dot(b16(rp_ref[b, rows, :]), b16(Gs[b]), _NT) + y0_ref[b, rows, :]
        Gs = [_mm3(Gs[b], tm_ref[b * nch + c]) + g0_ref[b * nch + c] for b in range(nb)]
    for b in range(nb):
        g_ref[b] = Gs[b]

    for s, (b, ln) in enumerate(seqs):
        y = y0_ref[s]
        yc = y - segsum(y) * (1.0 / HEAD_DIM)
        var = segsum(yc * yc) * (1.0 / HEAD_DIM)
        y_ref[b, :, ln] = y_ref[b, :, ln] + (yc * lax.rsqrt(var + GN_EPS) * gg_ref[:, ln] + gb_ref[:, ln])


def _cast_row_blocks(rows, nsteps):
    return max(nb for nb in range(1, nsteps + 1) if rows % nb == 0 and (rows // nb) % BF16_SUBLANES == 0)


def _wkv(rkv, lw, a, k_k, k_a, r_k, gn_g, gn_b, to_bf16=(), *, batch, seq, tc=256, tiles=5):
    n = RWKV_DIM
    shp = (batch, seq, n)
    tc = min(tc, seq)
    nch = tc // CHUNK
    nseq = batch * tiles
    wl = tiles * LANES
    masks, lvl16, tri = _wkv_masks()
    seq_spec = pl.BlockSpec((batch, tc, wl), lambda p, i: (0, i, p))
    rkv_spec = lambda which: pl.BlockSpec((None, batch, tc, wl), lambda p, i: (which, 0, i, p))
    par_spec = pl.BlockSpec((1, wl), lambda p, i: (0, p))
    rkv = rkv.reshape((3,) + shp)
    grid = (n // wl, seq // tc)
    cast_blocks = [_cast_row_blocks(w.shape[0], grid[0] * grid[1]) for w in to_bf16]

    def cast_spec(w, nblk):
        return pl.BlockSpec((w.shape[0] // nblk, w.shape[1]),
                            lambda p, i: (jnp.minimum(p * grid[1] + i, nblk - 1), 0))

    cast_specs = [cast_spec(w, nblk) for w, nblk in zip(to_bf16, cast_blocks)]
    out, *casted = pl.pallas_call(
        functools.partial(_wkv_body, cast_blocks=tuple(cast_blocks)),
        grid=grid,
        in_specs=[rkv_spec(0), seq_spec, rkv_spec(1), rkv_spec(2), seq_spec] + [par_spec] * 5 + [
            pl.BlockSpec(masks.shape, lambda p, i: (0, 0, 0)),
            pl.BlockSpec(lvl16.shape, lambda p, i: (0, 0, 0)),
            pl.BlockSpec(tri.shape, lambda p, i: (0, 0)),
        ] + cast_specs,
        out_specs=[seq_spec] + cast_specs,
        out_shape=[jax.ShapeDtypeStruct(shp, F32)] + [jax.ShapeDtypeStruct(w.shape, BF16) for w in to_bf16],
        scratch_shapes=[
            pltpu.VMEM((nseq, LANES, LANES), F32),
            pltpu.VMEM((nseq, tc, LANES), F32),
            pltpu.VMEM((nseq, tc, LANES), F32),
            pltpu.VMEM((nseq * nch, LANES, LANES), F32),
            pltpu.VMEM((nseq * nch, LANES, LANES), F32),
        ],
        compiler_params=_cparams(("parallel", "arbitrary"), 48),
        name="wkv",
    )(rkv, lw.reshape(shp), rkv, rkv, a.reshape(shp), k_k, k_a, r_k, gn_g, gn_b, masks, lvl16, tri, *to_bf16)
    return out.reshape(batch * seq, n), casted


ATTN_TILE = ATTN_BLOCK * max(DILATIONS)
ATTN_GROUP = 4
LOG2E = 1.4426950408889634


def _attn_bias():
    qi, kj = np.meshgrid(np.arange(ATTN_BLOCK), np.arange(2 * ATTN_BLOCK), indexing="ij")
    band = (kj >= qi) & (kj <= qi + ATTN_BLOCK)
    return jnp.asarray(np.stack([np.where(band, 0.0, NEG_BIG), np.where(kj >= ATTN_BLOCK, 0.0, NEG_BIG)]), F32)


def _dil_attn_body(q_ref, kc_ref, vc_ref, kp_ref, vp_ref, bias_ref, o_ref, m_ref, l_ref):
    blk = ATTN_BLOCK
    tile = q_ref.shape[0]
    first = jnp.where(pl.program_id(2) == 0, 1.0, 0.0)
    bias_first = bias_ref[0] + first * bias_ref[1]

    lane = lax.broadcasted_iota(jnp.int32, (1, LANES), 1)
    h0 = lane < HEAD_DIM

    def rows(d, rho, n):
        start = rho + d * blk * n
        return slice(start, start + blk) if d == 1 else pl.ds(start, blk, stride=d)

    def scores(d, nblk, rho, n):
        q = q_ref[rows(d, rho, n), :].astype(BF16)
        if n == 0:
            kp, bias = kp_ref[rows(d, rho, nblk - 1), :], bias_first
        else:
            kp, bias = kc_ref[rows(d, rho, n - 1), :], bias_ref[0]
        keys = jnp.concatenate([kp, kc_ref[rows(d, rho, n), :]], 0).astype(BF16)
        zero = jnp.zeros_like(q)
        return [lax.dot_general(jnp.where(hm, q, zero), keys, _NT, preferred_element_type=F32) + bias
                for hm in (h0, jnp.logical_not(h0))]

    def softmax(ss):
        ms = [jnp.max(s, axis=-1, keepdims=True) for s in ss]
        ps = [jnp.exp2(s - m) for s, m in zip(ss, ms)]
        ls = [jnp.sum(p, axis=-1, keepdims=True) for p in ps]
        return (jnp.concatenate([p.astype(BF16) for p in ps], 1),
                jnp.where(h0, ms[0], ms[1]), jnp.where(h0, ls[0], ls[1]))

    def weighted(d, nblk, rho, n, p):
        vp = vp_ref[rows(d, rho, nblk - 1), :] if n == 0 else vc_ref[rows(d, rho, n - 1), :]
        vals = jnp.concatenate([vp, vc_ref[rows(d, rho, n), :]], 0).astype(BF16)
        zv = jnp.zeros_like(vals)
        vbd = jnp.concatenate([jnp.where(h0, vals, zv), jnp.where(h0, zv, vals)], 0)
        return jnp.dot(p, vbd, preferred_element_type=F32)

    for g, d in enumerate(DILATIONS):
        nblk = tile // (blk * d)
        items = [(rho, n) for rho in range(d) for n in range(nblk)]
        for i0 in range(0, len(items), ATTN_GROUP):
            grp = items[i0:i0 + ATTN_GROUP]
            ss = [scores(d, nblk, rho, n) for rho, n in grp]
            sm = [softmax(s) for s in ss]
            os_ = [weighted(d, nblk, rho, n, p) for (rho, n), (p, _, _) in zip(grp, sm)]
            for (rho, n), (_, mb, lb), o in zip(grp, sm, os_):
                out_rows = rows(d, rho, n)
                if g == 0:
                    o_ref[out_rows, :] = o
                    m_ref[out_rows, :] = mb
                    l_ref[out_rows, :] = lb
                else:
                    m_old = m_ref[out_rows, :]
                    m_new = jnp.maximum(m_old, mb)
                    a_old = jnp.exp2(m_old - m_new)
                    a_blk = jnp.exp2(mb - m_new)
                    o_ref[out_rows, :] = o_ref[out_rows, :] * a_old + o * a_blk
                    l_ref[out_rows, :] = l_ref[out_rows, :] * a_old + lb * a_blk
                    m_ref[out_rows, :] = m_new
    o_ref[...] = o_ref[...] / l_ref[...]


def _dil_attn(qkv, *, batch, seq):
    assert seq % ATTN_TILE == 0
    x = qkv.reshape(3, batch, seq, ATTN_DIM)
    bias = _attn_bias()

    def cur(which):
        return pl.BlockSpec((None, None, ATTN_TILE, LANES), lambda b, p, i: (which, b, i, p))

    def prev(which):
        return pl.BlockSpec((None, None, ATTN_TILE, LANES), lambda b, p, i: (which, b, jnp.maximum(i - 1, 0), p))

    o = pl.pallas_call(
        _dil_attn_body,
        grid=(batch, ATTN_DIM // LANES, seq // ATTN_TILE),
        in_specs=[cur(0), cur(1), cur(2), prev(1), prev(2),
                  pl.BlockSpec(bias.shape, lambda b, p, i: (0, 0, 0))],
        out_specs=pl.BlockSpec((None, ATTN_TILE, LANES), lambda b, p, i: (b, i, p)),
        out_shape=jax.ShapeDtypeStruct((batch, seq, ATTN_DIM), F32),
        scratch_shapes=[pltpu.VMEM((ATTN_TILE, LANES), F32)] * 2,
        compiler_params=_cparams(("parallel", "parallel", "arbitrary"), 40),
        name="dil_attn",
    )(x, x, x, x, x, bias)
    return o.reshape(batch * seq, ATTN_DIM)


def _out_ln_body(yw_ref, gl_ref, ya_ref, h_ref, g2_ref, wr_ref, wa_ref, g_ref, b_ref, out_ref):
    gate = jnp.dot(jax.nn.sigmoid(gl_ref[...]).astype(BF16), g2_ref[...], preferred_element_type=F32)
    yr = yw_ref[...] * gate
    mix = (jnp.dot(yr.astype(BF16), wr_ref[...], preferred_element_type=F32)
           + jnp.dot(ya_ref[...].astype(BF16), wa_ref[...], preferred_element_type=F32))
    z = ALPHA * h_ref[...] + mix
    out_ref[...] = _layer_norm(z, g_ref[...], b_ref[...])


def _out_ln(yw, glo, ya, h, g2, wo_r, wo_a, g, b, *, tm=512):
    m, d = h.shape
    rows = lambda w: pl.BlockSpec((tm, w), lambda i: (i, 0))
    full = lambda a: pl.BlockSpec(a.shape, lambda i: (0, 0))
    return pl.pallas_call(
        _out_ln_body,
        grid=(m // tm,),
        in_specs=[rows(RWKV_DIM), rows(GATE_LORA), rows(ATTN_DIM), rows(d),
                  full(g2), full(wo_r), full(wo_a), full(g), full(b)],
        out_specs=rows(d),
        out_shape=jax.ShapeDtypeStruct((m, d), F32),
        compiler_params=_cparams(("parallel",), 48),
        name="out_ln",
    )(yw, glo, ya, h, g2, wo_r, wo_a, g, b)


def _pad_rows(w, rows):
    return jnp.pad(w, ((0, rows - w.shape[0]), (0, 0)))


def _pad_cols(w, cols):
    return jnp.pad(w, ((0, 0), (0, cols - w.shape[1])))


def kernel(x, positions, ffn1_w_gate, ffn1_w_up, ffn1_w_down, ln1_g, ln1_b, w_in, mu_r, mu_k, mu_v, mu_w, mu_a, mu_g, w0, w2, a0, a2, g2, k_k, k_a, r_k, gn_g, gn_b, w_out, ln2_g, ln2_b, ffn2_w_gate, ffn2_w_up, ffn2_w_down, ln3_g, ln3_b):
    batch, seq, d = x.shape
    m = batch * seq
    h = x.reshape(m, d)
    pos = positions.astype(F32).reshape(m, 1)
    c = RWKV_DIM
    for l in range(DEPTH):
        row = lambda p: p[l].reshape(1, -1)
        h = _ffn_ln(h, ffn1_w_gate[l].astype(BF16), ffn1_w_up[l].astype(BF16),
                    ffn1_w_down[l].astype(BF16), row(ln1_g), row(ln1_b))

        wi = w_in[l]
        o_lo = 3 * c
        o_q = o_lo + DECAY_LORA + ICLR_LORA + GATE_LORA
        w_rwkv = jnp.concatenate([
            wi[:, :o_lo],
            _pad_cols(wi[:, o_lo:o_lo + DECAY_LORA], LANES),
            _pad_cols(wi[:, o_lo + DECAY_LORA:o_lo + DECAY_LORA + ICLR_LORA], LANES),
            wi[:, o_lo + DECAY_LORA + ICLR_LORA:o_q]], axis=1).astype(BF16)
        mu_rwkv = jnp.concatenate([row(mu_r), row(mu_k), row(mu_v), _pad_cols(row(mu_w), LANES),
                                   _pad_cols(row(mu_a), LANES), row(mu_g)], axis=1)
        w_qkv = wi[:, o_q:].astype(BF16)

        qkv = _qkv_rope(h, w_qkv, pos)
        rkv_s, lw, a_s, glo = _rwkv_proj(h, w_rwkv, mu_rwkv, row(w0), _pad_rows(w2[l], LANES).astype(BF16),
                                         row(a0), _pad_rows(a2[l], LANES).astype(BF16), seq=seq)
        yw, ffn2_w = _wkv(rkv_s, lw, a_s, row(k_k), row(k_a), row(r_k), row(gn_g), row(gn_b),
                          (ffn2_w_gate[l], ffn2_w_up[l], ffn2_w_down[l]), batch=batch, seq=seq)
        ya = _dil_attn(qkv, batch=batch, seq=seq)

        wo = w_out[l].astype(BF16)
        h = _out_ln(yw, glo, ya, h, g2[l].astype(BF16), wo[:c], wo[c:], row(ln2_g), row(ln2_b))

        h = _ffn_ln(h, *ffn2_w, row(ln3_g), row(ln3_b))
    return h.reshape(batch, seq, d)
```

```python
import functools

import numpy as np

import jax
import jax.numpy as jnp
from jax import lax
from jax.experimental import pallas as pl
from jax.experimental.pallas import tpu as pltpu

F32 = jnp.float32
BF16 = jnp.bfloat16

D_MODEL = 2048
HEAD_DIM = 64
ATTN_HEADS = 12
RWKV_HEADS = 20
ATTN_DIM = ATTN_HEADS * HEAD_DIM
RWKV_DIM = RWKV_HEADS * HEAD_DIM
DECAY_LORA = 96
ICLR_LORA = 96
GATE_LORA = 256
DILATIONS = (1, 4, 16)
ATTN_BLOCK = 128
ROPE_THETA = 500000.0
ROT_DIM = HEAD_DIM // 4
D_FF = 5632
DEPTH = 1
ALPHA = (2 * DEPTH) ** 0.25
LN_EPS = 1e-5
GN_EPS = 64e-5

LANES = 128
SUBLANES = 8
BF16_SUBLANES = 16
LORA_PAD = 512
CHUNK = 64
WKV_GROUPS = 4
NEG_BIG = -1e30
HI = lax.Precision.HIGHEST


def _cparams(sem, vmem_mb):
    return pltpu.CompilerParams(dimension_semantics=sem, vmem_limit_bytes=vmem_mb * 1024 * 1024)


def _layer_norm(z, g, b):
    mu = jnp.mean(z, axis=-1, keepdims=True)
    zc = z - mu
    var = jnp.mean(zc * zc, axis=-1, keepdims=True)
    return zc * lax.rsqrt(var + LN_EPS) * g + b


def _ffn_ln_body(x_ref, wg_ref, wu_ref, wd_ref, g_ref, b_ref, o_ref, xb_ref, *, nf):
    j = pl.program_id(1)

    @pl.when(j == 0)
    def _():
        xb_ref[...] = x_ref[...].astype(BF16)
        o_ref[...] = jnp.zeros_like(o_ref)

    xb = xb_ref[...]
    gate = jnp.dot(xb, wg_ref[...], preferred_element_type=F32)
    up = jnp.dot(xb, wu_ref[...], preferred_element_type=F32)
    hmid = (gate * jax.nn.sigmoid(gate)) * up
    o_ref[...] += jnp.dot(hmid.astype(BF16), wd_ref[...], preferred_element_type=F32)

    @pl.when(j == nf - 1)
    def _():
        z = ALPHA * x_ref[...] + 0.5 * o_ref[...]
        o_ref[...] = _layer_norm(z, g_ref[...], b_ref[...])


def _ffn_ln(x, wg, wu, wd, g, b, *, tm=512, tf=512):
    m, d = x.shape
    dff = wg.shape[1]
    nf = dff // tf
    return pl.pallas_call(
        functools.partial(_ffn_ln_body, nf=nf),
        grid=(m // tm, nf),
        in_specs=[
            pl.BlockSpec((tm, d), lambda i, j: (i, 0)),
            pl.BlockSpec((d, tf), lambda i, j: (0, j)),
            pl.BlockSpec((d, tf), lambda i, j: (0, j)),
            pl.BlockSpec((tf, d), lambda i, j: (j, 0)),
            pl.BlockSpec((1, d), lambda i, j: (0, 0)),
            pl.BlockSpec((1, d), lambda i, j: (0, 0)),
        ],
        out_specs=pl.BlockSpec((tm, d), lambda i, j: (i, 0)),
        out_shape=jax.ShapeDtypeStruct((m, d), F32),
        scratch_shapes=[pltpu.VMEM((tm, d), BF16)],
        compiler_params=_cparams(("parallel", "arbitrary"), 40),
        name="ffn_ln",
    )(x, wg, wu, wd, g, b)


def _qkv_rope_body(x_ref, w_ref, pos_ref, invf_ref, o_ref):
    xb = x_ref[...].astype(BF16)
    l64 = lax.broadcasted_iota(jnp.int32, (1, LANES), 1) & (HEAD_DIM - 1)
    lo = l64 < ROT_DIM // 2
    hi = l64 < ROT_DIM
    ang = pos_ref[...] * invf_ref[...]
    c = jnp.where(hi, jnp.cos(ang), 1.0)
    s = jnp.where(hi, jnp.sin(ang), 0.0)
    for which in range(3):
        y = jnp.dot(xb, w_ref[:, which * ATTN_DIM:(which + 1) * ATTN_DIM], preferred_element_type=F32)
        if which == 2:
            o_ref[which] = y
            continue
        for t in range(ATTN_DIM // LANES):
            yt = y[:, t * LANES:(t + 1) * LANES]
            up = pltpu.roll(yt, LANES - ROT_DIM // 2, axis=1)
            dn = pltpu.roll(yt, ROT_DIM // 2, axis=1)
            rot = jnp.where(lo, -up, jnp.where(hi, dn, 0.0))
            yt = yt * c + rot * s
            if which == 0:
                yt = yt * (HEAD_DIM ** -0.5 * LOG2E)
            o_ref[which, :, t * LANES:(t + 1) * LANES] = yt


def _qkv_rope(x, w, pos_f32, *, tm=512):
    m, d = x.shape
    n = w.shape[1] // 3
    half = ROT_DIM // 2
    inv_freq = jnp.power(ROPE_THETA, -jnp.arange(half, dtype=F32) * (2.0 / ROT_DIM))
    l64 = jnp.arange(LANES) % HEAD_DIM
    invf = jnp.where(l64 < ROT_DIM, inv_freq[l64 % half], 0.0).astype(F32)[None, :]
    return pl.pallas_call(
        _qkv_rope_body,
        grid=(m // tm,),
        in_specs=[
            pl.BlockSpec((tm, d), lambda i: (i, 0)),
            pl.BlockSpec(w.shape, lambda i: (0, 0), pipeline_mode=pl.Buffered(1)),
            pl.BlockSpec((tm, 1), lambda i: (i, 0)),
            pl.BlockSpec((1, LANES), lambda i: (0, 0)),
        ],
        out_specs=pl.BlockSpec((3, tm, n), lambda i: (0, i, 0)),
        out_shape=jax.ShapeDtypeStruct((3, m, n), F32),
        compiler_params=_cparams(("parallel",), 40),
        name="qkv_rope",
    )(x, w, pos_f32, invf)


def _rwkv_proj_body(x_ref, w_ref, mu_ref, w0_ref, w2_ref, a0_ref, a2_ref,
                    rkv_ref, lw_ref, ao_ref, go_ref, carry_ref, *, tiles_per_seq):
    tm = x_ref.shape[0]
    n = RWKV_DIM
    row0 = lax.broadcasted_iota(jnp.int32, (tm, 1), 0) == 0

    @pl.when((pl.program_id(0) % tiles_per_seq) == 0)
    def _():
        carry_ref[...] = jnp.zeros_like(carry_ref)

    xb = x_ref[...].astype(BF16)

    def shifted(lo, hi):
        z = jnp.dot(xb, w_ref[:, lo:hi], preferred_element_type=F32)
        prev = jnp.where(row0, carry_ref[0:1, lo:hi], pltpu.roll(z, 1, axis=0))
        carry_ref[0:1, lo:hi] = z[tm - 1:tm, :]
        return z + (prev - z) * mu_ref[:, lo:hi]

    for which in range(3):
        rkv_ref[which] = shifted(which * n, (which + 1) * n)
    ls = shifted(3 * n, 3 * n + LORA_PAD)
    w_lo = ls[:, 0:LANES]
    a_lo = ls[:, LANES:2 * LANES]
    go_ref[...] = ls[:, 2 * LANES:]
    w = w0_ref[...] + jnp.dot(jnp.tanh(w_lo).astype(BF16), w2_ref[...], preferred_element_type=F32)
    x = -w
    softplus = jnp.maximum(x, 0.0) + jnp.log(1.0 + jnp.exp(-jnp.abs(x)))
    lw_ref[...] = -jnp.exp(-softplus - 0.5)
    a = a0_ref[...] + jnp.dot(a_lo.astype(BF16), a2_ref[...], preferred_element_type=F32)
    ao_ref[...] = jax.nn.sigmoid(a)


def _rwkv_proj(x, w_all, mu_all, w0, w2p, a0, a2p, *, seq, tm=256):
    m, d = x.shape
    n = RWKV_DIM
    const = lambda a: pl.BlockSpec(a.shape, lambda i: (0, 0), pipeline_mode=pl.Buffered(1))
    return pl.pallas_call(
        functools.partial(_rwkv_proj_body, tiles_per_seq=seq // tm),
        grid=(m // tm,),
        in_specs=[pl.BlockSpec((tm, d), lambda i: (i, 0)),
                  const(w_all), const(mu_all), const(w0), const(w2p), const(a0), const(a2p)],
        out_specs=[pl.BlockSpec((3, tm, n), lambda i: (0, i, 0))]
        + [pl.BlockSpec((tm, n), lambda i: (i, 0))] * 2
        + [pl.BlockSpec((tm, GATE_LORA), lambda i: (i, 0))],
        out_shape=[jax.ShapeDtypeStruct((3, m, n), F32)]
        + [jax.ShapeDtypeStruct((m, n), F32)] * 2
        + [jax.ShapeDtypeStruct((m, GATE_LORA), F32)],
        scratch_shapes=[pltpu.VMEM((8, w_all.shape[1]), F32)],
        compiler_params=_cparams(("arbitrary",), 48),
        name="rwkv_proj",
    )(x, w_all, mu_all, w0, w2p, a0, a2p)


_NN = (((1,), (0,)), ((), ()))
_NT = (((1,), (1,)), ((), ()))
_TN = (((0,), (0,)), ((), ()))


def _split(a):
    hi = a.astype(BF16)
    return hi, (a - hi.astype(F32)).astype(BF16)


def _dot(a, b, dims=_NN):
    return lax.dot_general(a, b, dims, preferred_element_type=F32)


def _mm3(a, b, dims=_NN):
    ah, al = _split(a)
    bh, bl = _split(b)
    return _dot(ah, bh, dims) + (_dot(ah, bl, dims) + _dot(al, bh, dims))


def _cumsum_rows(tri, x):
    hi, lo = _split(x)
    lo2 = (x - hi.astype(F32) - lo.astype(F32)).astype(BF16)
    return _dot(tri, hi) + (_dot(tri, lo) + _dot(tri, lo2))


_MK_M0, _MK_M1, _MK_TL, _MK_BR, _MK_EYE, _MK_HEAD, _MK_LVL = 0, 1, 2, 3, 4, 5, 6


def _wkv_masks():
    C = CHUNK
    ri, ci = np.meshgrid(np.arange(2 * C), np.arange(2 * C), indexing="ij")
    ti, sj = ri % C, ci % C
    top, left = ri < C, ci < C
    masks = [
        (sj < ti) | (~top & (sj == ti)),
        (sj < ti) | (top & (sj == ti)),
        top & left,
        ~top & ~left,
        ri == ci,
        (ri < HEAD_DIM) == (ci < HEAD_DIM),
    ]
    s = 1
    while s < C:
        masks.append((ri // (2 * s) == ci // (2 * s)) & ((ri // s) % 2 == 1) & ((ci // s) % 2 == 0))
        s *= 2
    tri = np.tril(np.ones((C, C), np.float32))
    masks = np.stack(masks).astype(np.float32)
    return jnp.asarray(masks), jnp.asarray(masks[_MK_LVL:], dtype=BF16), jnp.asarray(tri, dtype=BF16)


def _wkv_body(r_ref, lw_ref, k_ref, v_ref, a_ref, kk_ref, ka_ref, rk_ref, gg_ref, gb_ref,
              mk_ref, mk16_ref, tri_ref, *refs, cast_blocks):
    ncast = len(cast_blocks)
    cast_in, y_ref, cast_out = refs[:ncast], refs[ncast], refs[ncast + 1:2 * ncast + 1]
    g_ref, rp_ref, y0_ref, tm_ref, g0_ref = refs[2 * ncast + 1:]
    step = pl.program_id(0) * pl.num_programs(1) + pl.program_id(1)
    for src, dst, nblk in zip(cast_in, cast_out, cast_blocks):
        @pl.when(step < nblk)
        def _():
            dst[...] = src[...].astype(BF16)

    C = CHUNK
    tc = r_ref.shape[1]
    nch = tc // C
    seqs = [(b, slice(q * LANES, (q + 1) * LANES))
            for b in range(r_ref.shape[0]) for q in range(r_ref.shape[2] // LANES)]
    nb = len(seqs)

    @pl.when(pl.program_id(1) == 0)
    def _():
        g_ref[...] = jnp.zeros_like(g_ref)

    lane = lax.broadcasted_iota(jnp.int32, (1, LANES), 1)
    h0 = lane < HEAD_DIM
    m0 = jnp.where(h0, 1.0, 0.0)
    m1 = 1.0 - m0
    zeros_c = jnp.zeros((C, LANES), BF16)
    tri = tri_ref[...]
    mk = lambda i: mk_ref[i]

    def segsum(x):
        s0 = jnp.sum(x * m0, axis=-1, keepdims=True)
        s1 = jnp.sum(x * m1, axis=-1, keepdims=True)
        return jnp.where(h0, s0, s1)

    b16 = lambda x: x.astype(BF16)
    cat0 = lambda *xs: jnp.concatenate(xs, 0)
    cat1 = lambda *xs: jnp.concatenate(xs, 1)

    each = lambda f, *ls: [f(*xs) for xs in zip(*ls)]

    def y_rhs(ap, xw, x0, x1):
        return cat0(cat1(b16(ap * m0), b16(xw * m0)), cat1(zeros_c, x0), cat1(zeros_c, x1),
                    cat1(b16(ap * m1), b16(xw * m1)))

    def prep(s, c):
        b, ln = seqs[s]
        rows = slice(c * C, (c + 1) * C)
        r, ks, v, a = r_ref[b, rows, ln], k_ref[b, rows, ln], v_ref[b, rows, ln], a_ref[b, rows, ln]
        lw = lw_ref[b, rows, ln]
        kk = ks * kk_ref[:, ln]
        kk = kk / jnp.maximum(jnp.sqrt(segsum(kk * kk)), 1e-12)
        kmod = ks * (1.0 + (a - 1.0) * ka_ref[:, ln])
        y_ref[b, rows, ln] = segsum(r * kmod * rk_ref[:, ln]) * v
        bv = kk * a
        cs = _cumsum_rows(tri, lw)
        cl = cs[C - 1:C, :]
        At = -kk * jnp.exp(cs - lw)
        Rt = r * jnp.exp(cs)
        pinv = jnp.exp(-cs)
        pend = jnp.exp(cl - cs)
        return dict(s=s, c=c, Rt=Rt, cl=cl, Bt=b16(bv * pinv), Kt=b16(kmod * pinv),
                    BKh=b16(cat0(bv * pend, kmod * pend)), At0=b16(At * m0), At1=b16(At * m1),
                    Rt0=b16(Rt * m0), Rt1=b16(Rt * m1), v0=b16(v * m0), v1=b16(v * m1), vb=b16(v))

    def stages(P):
        get = lambda k: [p[k] for p in P]
        At0, At1, v0, v1 = get("At0"), get("At1"), get("v0"), get("v1")
        M0 = each(lambda p: _dot(cat0(p["At0"], p["Rt0"]), cat0(p["Bt"], p["Kt"]), _NT), P)
        yield
        M1 = each(lambda p: _dot(cat0(p["Rt1"], p["At1"]), cat0(p["Kt"], p["Bt"]), _NT), P)
        yield
        M0 = each(lambda x: x * mk(_MK_M0), M0)
        M1 = each(lambda x: x * mk(_MK_M1), M1)
        M0b, M1b = each(b16, M0), each(b16, M1)

        Lbd = each(lambda x0, x1: x0 * mk(_MK_TL) + x1 * mk(_MK_BR), M0, M1)
        Lbd16 = each(b16, Lbd)
        X = each(lambda l: mk(_MK_EYE) + l * mk(_MK_LVL), Lbd)
        for lv in range(1, C.bit_length() - 1):
            s = 1 << lv
            Xb = each(b16, X)
            Cl = each(lambda l: l * mk16_ref[lv], Lbd16)
            if s < SUBLANES:
                XC = each(lambda xb, cl: b16(_dot(xb, cl)), Xb, Cl)
                yield
                X = each(lambda x, xc, xb: x + _dot(xc, xb), X, XC, Xb)
            else:
                odd = lambda x: cat0(*[x[i:i + s] for i in range(s, 2 * C, 2 * s)])
                XC = each(lambda x, cl: b16(_dot(b16(odd(x)), cl)), X, Cl)
                yield
                upd = each(lambda xc, xb: _dot(xc, xb), XC, Xb)
                X = each(lambda x, u: cat0(*[piece for bi, i in enumerate(range(0, 2 * C, 2 * s))
                                             for piece in (x[i:i + s], x[i + s:i + 2 * s] + u[bi * s:(bi + 1) * s])]),
                         X, upd)
            yield
        Xb = each(b16, X)

        W = each(lambda a, b_, x0, x1: _dot(cat1(a[:C], b_[C:]), cat0(zeros_c, x0, x1, zeros_c)),
                 M0b, M1b, v0, v1)
        yield
        res = each(lambda xb, a0, a1, w: _dot(xb, cat0(cat1(a0, b16(w * m0)), cat1(a1, b16(w * m1)))),
                   Xb, At0, At1, W)
        yield
        Ap = each(lambda x: x[:C, :LANES] + x[C:, :LANES], res)
        XW = each(lambda x: x[:C, LANES:] + x[C:, LANES:], res)
        res_y = each(lambda a, b_, ap, xw, x0, x1: _dot(cat1(a[C:], b_[:C]), y_rhs(ap, xw, x0, x1)),
                     M0b, M1b, Ap, XW, v0, v1)
        yield
        res_t = each(lambda ap, xw, p: _dot(cat0(cat1(b16(ap), b16(xw)), cat1(zeros_c, p["vb"])), p["BKh"], _TN),
                     Ap, XW, P)
        for p, ry, rt in zip(P, res_y, res_t):
            s, c = p["s"], p["c"]
            rows = slice(c * C, (c + 1) * C)
            rp_ref[s, rows, :] = p["Rt"] + ry[:, :LANES]
            y0_ref[s, rows, :] = ry[:, LANES:]
            tm_ref[s * nch + c] = (rt[:LANES] + mk(_MK_EYE) * jnp.exp(p["cl"])) * mk(_MK_HEAD)
            g0_ref[s * nch + c] = rt[LANES:] * mk(_MK_HEAD)
        yield

    chains = [(s, c) for c in range(nch) for s in range(nb)]
    per_group = -(-len(chains) // WKV_GROUPS)
    groups = [chains[g0:g0 + per_group] for g0 in range(0, len(chains), per_group)]
    P = [prep(*ch) for ch in groups[0]]
    for gi in range(len(groups)):
        todo = list(groups[gi + 1]) if gi + 1 < len(groups) else []
        nxt = []
        for _ in stages(P):
            if todo:
                nxt.append(prep(*todo.pop(0)))
        nxt += [prep(*ch) for ch in todo]
        P = nxt

    Gs = [g_ref[b] for b in range(nb)]
    for c in range(nch):
        rows = slice(c * C, (c + 1) * C)
        for b in range(nb):
            y0_ref[b, rows, :] = _dot(b16(rp_ref[b, rows, :]), b16(Gs[b]), _NT) + y0_ref[b, rows, :]
        Gs = [_mm3(Gs[b], tm_ref[b * nch + c]) + g0_ref[b * nch + c] for b in range(nb)]
    for b in range(nb):
        g_ref[b] = Gs[b]

    for s, (b, ln) in enumerate(seqs):
        y = y0_ref[s]
        yc = y - segsum(y) * (1.0 / HEAD_DIM)
        var = segsum(yc * yc) * (1.0 / HEAD_DIM)
        y_ref[b, :, ln] = y_ref[b, :, ln] + (yc * lax.rsqrt(var + GN_EPS) * gg_ref[:, ln] + gb_ref[:, ln])


def _cast_row_blocks(rows, nsteps):
    return max(nb for nb in range(1, nsteps + 1) if rows % nb == 0 and (rows // nb) % BF16_SUBLANES == 0)


def _wkv(rkv, lw, a, k_k, k_a, r_k, gn_g, gn_b, to_bf16=(), *, batch, seq, tc=256, tiles=5):
    n = RWKV_DIM
    shp = (batch, seq, n)
    tc = min(tc, seq)
    nch = tc // CHUNK
    nseq = batch * tiles
    wl = tiles * LANES
    masks, lvl16, tri = _wkv_masks()
    seq_spec = pl.BlockSpec((batch, tc, wl), lambda p, i: (0, i, p))
    rkv_spec = lambda which: pl.BlockSpec((None, batch, tc, wl), lambda p, i: (which, 0, i, p))
    par_spec = pl.BlockSpec((1, wl), lambda p, i: (0, p))
    rkv = rkv.reshape((3,) + shp)
    grid = (n // wl, seq // tc)
    cast_blocks = [_cast_row_blocks(w.shape[0], grid[0] * grid[1]) for w in to_bf16]

    def cast_spec(w, nblk):
        return pl.BlockSpec((w.shape[0] // nblk, w.shape[1]),
                            lambda p, i: (jnp.minimum(p * grid[1] + i, nblk - 1), 0))

    cast_specs = [cast_spec(w, nblk) for w, nblk in zip(to_bf16, cast_blocks)]
    out, *casted = pl.pallas_call(
        functools.partial(_wkv_body, cast_blocks=tuple(cast_blocks)),
        grid=grid,
        in_specs=[rkv_spec(0), seq_spec, rkv_spec(1), rkv_spec(2), seq_spec] + [par_spec] * 5 + [
            pl.BlockSpec(masks.shape, lambda p, i: (0, 0, 0)),
            pl.BlockSpec(lvl16.shape, lambda p, i: (0, 0, 0)),
            pl.BlockSpec(tri.shape, lambda p, i: (0, 0)),
        ] + cast_specs,
        out_specs=[seq_spec] + cast_specs,
        out_shape=[jax.ShapeDtypeStruct(shp, F32)] + [jax.ShapeDtypeStruct(w.shape, BF16) for w in to_bf16],
        scratch_shapes=[
            pltpu.VMEM((nseq, LANES, LANES), F32),
            pltpu.VMEM((nseq, tc, LANES), F32),
            pltpu.VMEM((nseq, tc, LANES), F32),
            pltpu.VMEM((nseq * nch, LANES, LANES), F32),
            pltpu.VMEM((nseq * nch, LANES, LANES), F32),
        ],
        compiler_params=_cparams(("parallel", "arbitrary"), 48),
        name="wkv",
    )(rkv, lw.reshape(shp), rkv, rkv, a.reshape(shp), k_k, k_a, r_k, gn_g, gn_b, masks, lvl16, tri, *to_bf16)
    return out.reshape(batch * seq, n), casted


ATTN_TILE = ATTN_BLOCK * max(DILATIONS)
ATTN_GROUP = 2
LOG2E = 1.4426950408889634


def _attn_bias():
    qi, kj = np.meshgrid(np.arange(ATTN_BLOCK), np.arange(2 * ATTN_BLOCK), indexing="ij")
    band = (kj >= qi) & (kj <= qi + ATTN_BLOCK)
    return jnp.asarray(np.stack([np.where(band, 0.0, NEG_BIG), np.where(kj >= ATTN_BLOCK, 0.0, NEG_BIG)]), F32)


def _dil_attn_body(q_ref, kc_ref, vc_ref, kp_ref, vp_ref, bias_ref, o_ref, m_ref, l_ref):
    blk = ATTN_BLOCK
    tile = q_ref.shape[0]
    first = jnp.where(pl.program_id(2) == 0, 1.0, 0.0)
    bias_first = bias_ref[0] + first * bias_ref[1]

    lane = lax.broadcasted_iota(jnp.int32, (1, LANES), 1)
    h0 = lane < HEAD_DIM

    def rows(d, rho, n):
        start = rho + d * blk * n
        return slice(start, start + blk) if d == 1 else pl.ds(start, blk, stride=d)

    def scores(d, nblk, rho, n):
        q = q_ref[rows(d, rho, n), :].astype(BF16)
        if n == 0:
            kp, bias = kp_ref[rows(d, rho, nblk - 1), :], bias_first
        else:
            kp, bias = kc_ref[rows(d, rho, n - 1), :], bias_ref[0]
        keys = jnp.concatenate([kp, kc_ref[rows(d, rho, n), :]], 0).astype(BF16)
        zero = jnp.zeros_like(q)
        return [lax.dot_general(jnp.where(hm, q, zero), keys, _NT, preferred_element_type=F32) + bias
                for hm in (h0, jnp.logical_not(h0))]

    def softmax(ss):
        ms = [jnp.max(s, axis=-1, keepdims=True) for s in ss]
        ps = [jnp.exp2(s - m) for s, m in zip(ss, ms)]
        ls = [jnp.sum(p, axis=-1, keepdims=True) for p in ps]
        return (jnp.concatenate([p.astype(BF16) for p in ps], 1),
                jnp.where(h0, ms[0], ms[1]), jnp.where(h0, ls[0], ls[1]))

    def weighted(d, nblk, rho, n, p):
        vp = vp_ref[rows(d, rho, nblk - 1), :] if n == 0 else vc_ref[rows(d, rho, n - 1), :]
        vals = jnp.concatenate([vp, vc_ref[rows(d, rho, n), :]], 0).astype(BF16)
        zv = jnp.zeros_like(vals)
        vbd = jnp.concatenate([jnp.where(h0, vals, zv), jnp.where(h0, zv, vals)], 0)
        return jnp.dot(p, vbd, preferred_element_type=F32)

    groups = []
    for g, d in enumerate(DILATIONS):
        nblk = tile // (blk * d)
        items = [(rho, n) for rho in range(d) for n in range(nblk)]
        groups += [(g, d, nblk, items[i0:i0 + ATTN_GROUP]) for i0 in range(0, len(items), ATTN_GROUP)]

    def finish(g, d, nblk, grp, ss):
        sm = [softmax(s) for s in ss]
        os_ = [weighted(d, nblk, rho, n, p) for (rho, n), (p, _, _) in zip(grp, sm)]
        for (rho, n), (_, mb, lb), o in zip(grp, sm, os_):
            out_rows = rows(d, rho, n)
            if g == 0:
                o_ref[out_rows, :] = o
                m_ref[out_rows, :] = mb
                l_ref[out_rows, :] = lb
            else:
                m_old = m_ref[out_rows, :]
                m_new = jnp.maximum(m_old, mb)
                a_old = jnp.exp2(m_old - m_new)
                a_blk = jnp.exp2(mb - m_new)
                o_ref[out_rows, :] = o_ref[out_rows, :] * a_old + o * a_blk
                l_ref[out_rows, :] = l_ref[out_rows, :] * a_old + lb * a_blk
                m_ref[out_rows, :] = m_new

    pending = None
    for g, d, nblk, grp in groups:
        ss = [scores(d, nblk, rho, n) for rho, n in grp]
        if pending is not None:
            finish(*pending)
        pending = (g, d, nblk, grp, ss)
    finish(*pending)
    o_ref[...] = o_ref[...] / l_ref[...]


def _dil_attn(qkv, *, batch, seq):
    assert seq % ATTN_TILE == 0
    x = qkv.reshape(3, batch, seq, ATTN_DIM)
    bias = _attn_bias()

    def cur(which):
        return pl.BlockSpec((None, None, ATTN_TILE, LANES), lambda b, p, i: (which, b, i, p))

    def prev(which):
        return pl.BlockSpec((None, None, ATTN_TILE, LANES), lambda b, p, i: (which, b, jnp.maximum(i - 1, 0), p))

    o = pl.pallas_call(
        _dil_attn_body,
        grid=(batch, ATTN_DIM // LANES, seq // ATTN_TILE),
        in_specs=[cur(0), cur(1), cur(2), prev(1), prev(2),
                  pl.BlockSpec(bias.shape, lambda b, p, i: (0, 0, 0))],
        out_specs=pl.BlockSpec((None, ATTN_TILE, LANES), lambda b, p, i: (b, i, p)),
        out_shape=jax.ShapeDtypeStruct((batch, seq, ATTN_DIM), F32),
        scratch_shapes=[pltpu.VMEM((ATTN_TILE, LANES), F32)] * 2,
        compiler_params=_cparams(("parallel", "parallel", "arbitrary"), 40),
        name="dil_attn",
    )(x, x, x, x, x, bias)
    return o.reshape(batch * seq, ATTN_DIM)


def _out_ln_body(yw_ref, gl_ref, ya_ref, h_ref, g2_ref, wr_ref, wa_ref, g_ref, b_ref, out_ref):
    half = out_ref.shape[0] // 2
    for rs in (slice(0, half), slice(half, 2 * half)):
        gate = jnp.dot(jax.nn.sigmoid(gl_ref[rs, :]).astype(BF16), g2_ref[...], preferred_element_type=F32)
        yr = yw_ref[rs, :] * gate
        mix = (jnp.dot(yr.astype(BF16), wr_ref[...], preferred_element_type=F32)
               + jnp.dot(ya_ref[rs, :].astype(BF16), wa_ref[...], preferred_element_type=F32))
        z = ALPHA * h_ref[rs, :] + mix
        out_ref[rs, :] = _layer_norm(z, g_ref[...], b_ref[...])


def _out_ln(yw, glo, ya, h, g2, wo_r, wo_a, g, b, *, tm=512):
    m, d = h.shape
    rows = lambda w: pl.BlockSpec((tm, w), lambda i: (i, 0))
    full = lambda a: pl.BlockSpec(a.shape, lambda i: (0, 0))
    return pl.pallas_call(
        _out_ln_body,
        grid=(m // tm,),
        in_specs=[rows(RWKV_DIM), rows(GATE_LORA), rows(ATTN_DIM), rows(d),
                  full(g2), full(wo_r), full(wo_a), full(g), full(b)],
        out_specs=rows(d),
        out_shape=jax.ShapeDtypeStruct((m, d), F32),
        compiler_params=_cparams(("parallel",), 48),
        name="out_ln",
    )(yw, glo, ya, h, g2, wo_r, wo_a, g, b)


def _pad_rows(w, rows):
    return jnp.pad(w, ((0, rows - w.shape[0]), (0, 0)))


def _pad_cols(w, cols):
    return jnp.pad(w, ((0, 0), (0, cols - w.shape[1])))


def kernel(x, positions, ffn1_w_gate, ffn1_w_up, ffn1_w_down, ln1_g, ln1_b, w_in, mu_r, mu_k, mu_v, mu_w, mu_a, mu_g, w0, w2, a0, a2, g2, k_k, k_a, r_k, gn_g, gn_b, w_out, ln2_g, ln2_b, ffn2_w_gate, ffn2_w_up, ffn2_w_down, ln3_g, ln3_b):
    batch, seq, d = x.shape
    m = batch * seq
    h = x.reshape(m, d)
    pos = positions.astype(F32).reshape(m, 1)
    c = RWKV_DIM
    for l in range(DEPTH):
        row = lambda p: p[l].reshape(1, -1)
        h = _ffn_ln(h, ffn1_w_gate[l].astype(BF16), ffn1_w_up[l].astype(BF16),
                    ffn1_w_down[l].astype(BF16), row(ln1_g), row(ln1_b))

        wi = w_in[l]
        o_lo = 3 * c
        o_q = o_lo + DECAY_LORA + ICLR_LORA + GATE_LORA
        w_rwkv = jnp.concatenate([
            wi[:, :o_lo],
            _pad_cols(wi[:, o_lo:o_lo + DECAY_LORA], LANES),
            _pad_cols(wi[:, o_lo + DECAY_LORA:o_lo + DECAY_LORA + ICLR_LORA], LANES),
            wi[:, o_lo + DECAY_LORA + ICLR_LORA:o_q]], axis=1).astype(BF16)
        mu_rwkv = jnp.concatenate([row(mu_r), row(mu_k), row(mu_v), _pad_cols(row(mu_w), LANES),
                                   _pad_cols(row(mu_a), LANES), row(mu_g)], axis=1)
        w_qkv = wi[:, o_q:].astype(BF16)

        qkv = _qkv_rope(h, w_qkv, pos)
        rkv_s, lw, a_s, glo = _rwkv_proj(h, w_rwkv, mu_rwkv, row(w0), _pad_rows(w2[l], LANES).astype(BF16),
                                         row(a0), _pad_rows(a2[l], LANES).astype(BF16), seq=seq)
        yw, ffn2_w = _wkv(rkv_s, lw, a_s, row(k_k), row(k_a), row(r_k), row(gn_g), row(gn_b),
                          (ffn2_w_gate[l], ffn2_w_up[l], ffn2_w_down[l]), batch=batch, seq=seq)
        ya = _dil_attn(qkv, batch=batch, seq=seq)

        wo = w_out[l].astype(BF16)
        h = _out_ln(yw, glo, ya, h, g2[l].astype(BF16), wo[:c], wo[c:], row(ln2_g), row(ln2_b))

        h = _ffn_ln(h, *ffn2_w, row(ln3_g), row(ln3_b))
    return h.reshape(batch, seq, d)
```
